```python
import jax, jax.numpy as jnp
from jax import lax
import numpy as np

D_MODEL = 1024
BATCH = 8
SEQ = 4096
DEPTH = 1

CHUNK = 64
EPS = 1e-6

POOL_WIDTH = D_MODEL
POOL_WINDOWS = (2, 4, 8, 16)
POOL_GROUPS = len(POOL_WINDOWS)
POOL_GROUP_DIM = POOL_WIDTH // POOL_GROUPS

GLA_HEADS = 4
GLA_KEY_DIM = D_MODEL // 2
GLA_VAL_DIM = D_MODEL
GLA_HEAD_K = GLA_KEY_DIM // GLA_HEADS
GLA_HEAD_V = GLA_VAL_DIM // GLA_HEADS
GLA_GATE_RANK = 16
GLA_GATE_NORMALIZER = 16.0

SPLITS = (
    POOL_WIDTH,
    POOL_WIDTH,
    GLA_KEY_DIM,
    GLA_KEY_DIM,
    GLA_VAL_DIM,
    GLA_VAL_DIM,
    GLA_GATE_RANK,
    D_MODEL,
    D_MODEL,
)
IN_WIDTH = sum(SPLITS)
SPLIT_POINTS = tuple(int(v) for v in np.cumsum(SPLITS)[:-1])

kernel_name = "hybrid_pool_gla_gated_merge"


def rmsnorm(x, gain):
    xf = x.astype(jnp.float32)
    y = xf * lax.rsqrt(jnp.mean(xf * xf, axis=-1, keepdims=True) + EPS)
    return (y * gain.astype(jnp.float32)).astype(x.dtype)


def causal_multiscale_pool(u):
    B, S, _ = u.shape
    ug = u.reshape(B, S, POOL_GROUPS, POOL_GROUP_DIM).astype(jnp.float32)
    cs = jnp.cumsum(ug, axis=1)
    t = jnp.arange(1, S + 1, dtype=jnp.float32)
    outs = []
    for gi, w in enumerate(POOL_WINDOWS):
        csg = cs[:, :, gi]
        prev = jnp.pad(csg, ((0, 0), (w, 0), (0, 0)))[:, :S]
        cnt = jnp.minimum(t, float(w))[None, :, None]
        outs.append((csg - prev) / cnt)
    mean = jnp.stack(outs, axis=2)
    return (mean - ug).astype(u.dtype)


def gla_chunked(q, k, v, log_a):
    B, S, H, dk = q.shape
    dv = v.shape[-1]
    N = S // CHUNK

    def to_chunks(t):
        return t.reshape(B, N, CHUNK, H, -1).transpose(0, 3, 1, 2, 4)

    qc = to_chunks(q.astype(jnp.float32)) * (dk ** -0.5)
    kc = to_chunks(k.astype(jnp.float32))
    vc = to_chunks(v.astype(jnp.float32))
    gc = to_chunks(log_a.astype(jnp.float32))
    b = jnp.cumsum(gc, axis=3)
    b_last = b[:, :, :, -1:, :]
    q_dec = qc * jnp.exp(b)
    k_inv = kc * jnp.exp(-b)
    k_to_end = kc * jnp.exp(b_last - b)
    decay_chunk = jnp.exp(b_last[:, :, :, 0, :])

    mask = jnp.tril(jnp.ones((CHUNK, CHUNK), dtype=bool))
    scores = jnp.einsum('bhnid,bhnjd->bhnij', q_dec, k_inv)
    scores = jnp.where(mask, scores, 0.0)
    o_intra = jnp.einsum('bhnij,bhnjv->bhniv', scores, vc)

    def step(state, inp):
        q_n, k_n, v_n, dec_n = inp
        o_n = jnp.einsum('bhid,bhdv->bhiv', q_n, state)
        state = state * dec_n[..., None] + jnp.einsum('bhjd,bhjv->bhdv', k_n, v_n)
        return state, o_n

    xs = (q_dec.transpose(2, 0, 1, 3, 4), k_to_end.transpose(2, 0, 1, 3, 4),
          vc.transpose(2, 0, 1, 3, 4), decay_chunk.transpose(2, 0, 1, 3))
    s0 = jnp.zeros((B, H, dk, dv), jnp.float32)
    _, o_inter = lax.scan(step, s0, xs)
    o = o_intra + o_inter.transpose(1, 2, 0, 3, 4)
    return o.transpose(0, 2, 3, 1, 4).reshape(B, S, H, dv).astype(q.dtype)


def hybrid_layer(x, c, g_norm, w_ada, b_ada, w_in, w_pool_group, pool_scale,
                 w_alpha_up, b_alpha, g_gla_head, w_pool_out, w_gla_out, w_out):
    B, S, D = x.shape
    mod = jax.nn.silu(c) @ w_ada + b_ada
    shift, scale, gate = jnp.split(mod, 3, axis=-1)
    h = rmsnorm(x, g_norm) * (1.0 + scale[:, None]) + shift[:, None]

    z = h @ w_in
    (pv, pg, q, k, v, gg, a_low, mg_pool, mg_gla) = jnp.split(z, SPLIT_POINTS, axis=-1)

    pooled = causal_multiscale_pool(pv)
    mixed = jnp.einsum('bsgc,gcd->bsgd', pooled, w_pool_group).reshape(B, S, POOL_WIDTH)
    y_pool = mixed * pool_scale * jax.nn.silu(pg)

    log_a = jax.nn.log_sigmoid((a_low @ w_alpha_up + b_alpha).astype(jnp.float32)) / GLA_GATE_NORMALIZER
    o = gla_chunked(q.reshape(B, S, GLA_HEADS, GLA_HEAD_K),
                    k.reshape(B, S, GLA_HEADS, GLA_HEAD_K),
                    v.reshape(B, S, GLA_HEADS, GLA_HEAD_V),
                    log_a.reshape(B, S, GLA_HEADS, GLA_HEAD_K))
    o = rmsnorm(o, g_gla_head).reshape(B, S, GLA_VAL_DIM)
    y_gla = o * jax.nn.silu(gg)

    merged = (jax.nn.sigmoid(mg_pool) * (y_pool @ w_pool_out)
              + jax.nn.sigmoid(mg_gla) * (y_gla @ w_gla_out))
    out = merged @ w_out
    return x + gate[:, None] * out


def setup_inputs(seed: int = 0) -> dict:
    key = jax.random.key(seed)
    ks = jax.random.split(key, 16)
    D = D_MODEL
    f32 = jnp.float32
    nrm = lambda k, shape, s: (jax.random.normal(k, shape, f32) * s)
    return {
        "x": nrm(ks[0], (BATCH, SEQ, D), 1.0),
        "c": nrm(ks[1], (BATCH, D), 1.0),
        "g_norm": 1.0 + nrm(ks[2], (DEPTH, D), 0.02),
        "w_ada": nrm(ks[3], (DEPTH, D, 3 * D), 0.5 * D ** -0.5),
        "b_ada": nrm(ks[4], (DEPTH, 3 * D), 0.02),
        "w_in": nrm(ks[5], (DEPTH, D, IN_WIDTH), D ** -0.5),
        "w_pool_group": nrm(ks[6], (DEPTH, POOL_GROUPS, POOL_GROUP_DIM, POOL_GROUP_DIM), POOL_GROUP_DIM ** -0.5),
        "pool_scale": 1.0 + nrm(ks[7], (DEPTH, POOL_WIDTH), 0.1),
        "w_alpha_up": nrm(ks[8], (DEPTH, GLA_GATE_RANK, GLA_KEY_DIM), GLA_GATE_RANK ** -0.5),
        "b_alpha": nrm(ks[9], (DEPTH, GLA_KEY_DIM), 0.1),
        "g_gla_head": 1.0 + nrm(ks[10], (DEPTH, GLA_HEAD_V), 0.02),
        "w_pool_out": nrm(ks[11], (DEPTH, POOL_WIDTH, D), POOL_WIDTH ** -0.5),
        "w_gla_out": nrm(ks[12], (DEPTH, GLA_VAL_DIM, D), GLA_VAL_DIM ** -0.5),
        "w_out": nrm(ks[13], (DEPTH, D, D), D ** -0.5),
        "g_final": 1.0 + nrm(ks[14], (D,), 0.02),
    }


def reference(x, c, g_norm, w_ada, b_ada, w_in, w_pool_group, pool_scale,
              w_alpha_up, b_alpha, g_gla_head, w_pool_out, w_gla_out, w_out, g_final):
    h = x
    for l in range(DEPTH):
        h = hybrid_layer(h, c, g_norm[l], w_ada[l], b_ada[l], w_in[l], w_pool_group[l],
                         pool_scale[l], w_alpha_up[l], b_alpha[l], g_gla_head[l],
                         w_pool_out[l], w_gla_out[l], w_out[l])
    return rmsnorm(h, g_final)
```

```python
import functools

import jax
import jax.numpy as jnp
from jax import lax
from jax.experimental import pallas as pl
from jax.experimental.pallas import tpu as pltpu

F32 = jnp.float32
BF16 = jnp.bfloat16

D_MODEL = 1024
EPS = 1e-6
POOL_WINDOWS = (2, 4, 8, 16)
POOL_GROUP_DIM = D_MODEL // len(POOL_WINDOWS)
POOL_HALO = 16
GLA_HEADS = 4
GLA_KEY_DIM = D_MODEL // 2
GLA_VAL_DIM = D_MODEL
GLA_HEAD_K = GLA_KEY_DIM // GLA_HEADS
GLA_HEAD_V = GLA_VAL_DIM // GLA_HEADS
GLA_GATE_RANK = 16
GLA_GATE_NORMALIZER = 16.0
GLA_BLOCK = 128
LANES = 128
ROW_TILE = 256
MOD_COL_TILE = 512
VMEM_LIMIT_BYTES = 60 * 1024 * 1024

_NT = (((1,), (1,)), ((), ()))


def _dot(a, b):
    return jnp.dot(a, b, preferred_element_type=F32)


def _sigmoid(x):
    return 1.0 / (1.0 + jnp.exp(-x))


def _log_sigmoid(x):
    return jnp.minimum(x, 0.0) - jnp.log1p(jnp.exp(-jnp.abs(x)))


def _mod_kernel(c_ref, w_ref, b_ref, o_ref):
    c = c_ref[...]
    s = c * _sigmoid(c)
    o_ref[...] = _dot(s.astype(BF16), w_ref[...].astype(BF16)) + b_ref[...]


def _layer_kernel(x_ref, mod_ref, gnorm_ref, wpv_ref, wpg_ref, wq_ref, wk_ref, wv_ref, wgg_ref, wa_ref,
                  wmgp_ref, wmgg_ref, wgrp_ref, pscale_ref, wup_ref, balpha_ref, ghead_ref, wpo_ref,
                  wgo_ref, wo_ref, gfin_ref, tri_ref, out_ref, st_ref, pvx_ref, o_scr):
    j = pl.program_id(1)
    ts = x_ref.shape[0]
    d = D_MODEL

    @pl.when(j == 0)
    def _():
        st_ref[...] = jnp.zeros_like(st_ref)
        pvx_ref[0:POOL_HALO, :] = jnp.zeros((POOL_HALO, d), F32)

    x = x_ref[...]
    mod = mod_ref[...]
    shift, scale, gate = mod[:, 0:d], mod[:, d:2 * d], mod[:, 2 * d:3 * d]
    h = x * lax.rsqrt(jnp.mean(x * x, axis=-1, keepdims=True) + EPS) * gnorm_ref[...]
    h = h * (1.0 + scale) + shift
    hb = h.astype(BF16)

    pv = _dot(hb, wpv_ref[...])
    pvx_ref[POOL_HALO:POOL_HALO + ts, :] = pv
    t_plus_1 = lax.broadcasted_iota(jnp.int32, (ts, 1), 0) + (j * ts + 1)
    mixed = []
    for g, w in enumerate(POOL_WINDOWS):
        cols = slice(g * POOL_GROUP_DIM, (g + 1) * POOL_GROUP_DIM)
        win = pv[:, cols]
        for s in range(1, w):
            win = win + pvx_ref[POOL_HALO - s:POOL_HALO - s + ts, cols]
        inv_cnt = 1.0 / jnp.minimum(t_plus_1, w).astype(F32)
        pooled = win * inv_cnt - pv[:, cols]
        mixed.append(_dot(pooled.astype(BF16), wgrp_ref[g]))
    mixed = jnp.concatenate(mixed, axis=1)
    pvx_ref[0:POOL_HALO, :] = pvx_ref[ts:ts + POOL_HALO, :]
    pg = _dot(hb, wpg_ref[...])
    y_pool = mixed * pscale_ref[...] * (pg * _sigmoid(pg))
    merged = _sigmoid(_dot(hb, wmgp_ref[...])) * _dot(y_pool.astype(BF16), wpo_ref[...])

    a_low = _dot(hb, wa_ref[...])
    log_a = _log_sigmoid(_dot(a_low.astype(BF16), wup_ref[...]) + balpha_ref[...]) * (1.0 / GLA_GATE_NORMALIZER)
    la_hi = log_a.astype(BF16)
    la_lo = (log_a - la_hi.astype(F32)).astype(BF16)
    tri = tri_ref[...]
    cum = _dot(tri, la_hi) + _dot(tri, la_lo)
    q = _dot(hb, wq_ref[...]) * (GLA_HEAD_K ** -0.5)
    k = _dot(hb, wk_ref[...])
    v = _dot(hb, wv_ref[...])
    rr = lax.broadcasted_iota(jnp.int32, (GLA_BLOCK, GLA_BLOCK), 0)
    cc = lax.broadcasted_iota(jnp.int32, (GLA_BLOCK, GLA_BLOCK), 1)
    causal = rr >= cc
    for r in range(ts // GLA_BLOCK):
        rows = slice(r * GLA_BLOCK, (r + 1) * GLA_BLOCK)
        cb = cum[rows, :]
        b_mid = cum[r * GLA_BLOCK + GLA_BLOCK // 2 - 1:r * GLA_BLOCK + GLA_BLOCK // 2, :]
        b_end = cum[(r + 1) * GLA_BLOCK - 1:(r + 1) * GLA_BLOCK, :]
        qb, kb = q[rows, :], k[rows, :]
        q_in = (qb * jnp.exp(cb - b_mid)).astype(BF16)
        k_in = (kb * jnp.exp(b_mid - cb)).astype(BF16)
        q_start = (qb * jnp.exp(cb)).astype(BF16)
        k_end = (kb * jnp.exp(b_end - cb)).astype(BF16)
        block_decay = jnp.exp(b_end)
        for hh in range(GLA_HEADS):
            kc = slice(hh * GLA_HEAD_K, (hh + 1) * GLA_HEAD_K)
            vc = slice(hh * GLA_HEAD_V, (hh + 1) * GLA_HEAD_V)
            scores = lax.dot_general(q_in[:, kc], k_in[:, kc], _NT, preferred_element_type=F32)
            scores = jnp.where(causal, scores, 0.0)
            vb = v[rows, vc]
            state_t = st_ref[hh]
            o = _dot(scores.astype(BF16), vb.astype(BF16))
            o = o + lax.dot_general(q_start[:, kc], state_t.astype(BF16), _NT, preferred_element_type=F32)
            o_scr[rows, vc] = o
            st_ref[hh] = state_t * block_decay[:, kc] + _dot(vb.T.astype(BF16), k_end[:, kc])

    o = o_scr[...]
    normed = []
    for hh in range(GLA_HEADS):
        oh = o[:, hh * GLA_HEAD_V:(hh + 1) * GLA_HEAD_V]
        normed.append(oh * lax.rsqrt(jnp.mean(oh * oh, axis=-1, keepdims=True) + EPS))
    gg = _dot(hb, wgg_ref[...])
    y_gla = jnp.concatenate(normed, axis=1) * ghead_ref[...] * (gg * _sigmoid(gg))
    merged = merged + _sigmoid(_dot(hb, wmgg_ref[...])) * _dot(y_gla.astype(BF16), wgo_ref[...])

    y = x + gate * _dot(merged.astype(BF16), wo_ref[...])
    out_ref[...] = y * lax.rsqrt(jnp.mean(y * y, axis=-1, keepdims=True) + EPS) * gfin_ref[...]


def _resident(shape):
    return pl.BlockSpec(shape, lambda b, j: (0,) * len(shape), pipeline_mode=pl.Buffered(1))


def _modulation(c, w_ada, b_ada):
    bsz, d = c.shape
    n = w_ada.shape[1]
    return pl.pallas_call(
        _mod_kernel,
        out_shape=jax.ShapeDtypeStruct((bsz, n), F32),
        grid=(n // MOD_COL_TILE,),
        in_specs=[pl.BlockSpec((bsz, d), lambda i: (0, 0)),
                  pl.BlockSpec((d, MOD_COL_TILE), lambda i: (0, i)),
                  pl.BlockSpec((1, MOD_COL_TILE), lambda i: (0, i))],
        out_specs=pl.BlockSpec((bsz, MOD_COL_TILE), lambda i: (0, i)),
        name="adaln_modulation",
    )(c, w_ada, b_ada.reshape(1, n))


def _layer(x, mod, g_norm, w_in, w_pool_group, pool_scale, w_alpha_up, b_alpha, g_gla_head, w_pool_out,
           w_gla_out, w_out, g_final, row_tile):
    bsz, seq, d = x.shape
    assert d == D_MODEL and seq % row_tile == 0 and row_tile % GLA_BLOCK == 0
    ts = row_tile

    widths = (d, d, GLA_KEY_DIM, GLA_KEY_DIM, GLA_VAL_DIM, GLA_VAL_DIM, GLA_GATE_RANK, d, d)
    parts, start = [], 0
    for wdt in widths:
        parts.append(w_in[:, start:start + wdt].astype(BF16))
        start += wdt
    wpv, wpg, wq, wk, wv, wgg, wa, wmgp, wmgg = parts
    wa = jnp.pad(wa, ((0, 0), (0, LANES - GLA_GATE_RANK)))
    wup = jnp.pad(w_alpha_up.astype(BF16), ((0, LANES - GLA_GATE_RANK), (0, 0)))
    rows = jnp.arange(ts)
    tri = ((rows[:, None] >= rows[None, :]) & (rows[:, None] // GLA_BLOCK == rows[None, :] // GLA_BLOCK)).astype(BF16)

    row = lambda a: a.reshape(1, -1).astype(F32)
    operands = [
        x, mod.reshape(bsz, 1, 3 * d), row(g_norm), wpv, wpg, wq, wk, wv, wgg, wa, wmgp, wmgg,
        w_pool_group.astype(BF16), row(pool_scale), wup, row(b_alpha), row(jnp.tile(g_gla_head, GLA_HEADS)),
        w_pool_out.astype(BF16), w_gla_out.astype(BF16), w_out.astype(BF16), row(g_final), tri,
    ]
    in_specs = [pl.BlockSpec((None, ts, d), lambda b, j: (b, j, 0)),
                pl.BlockSpec((None, 1, 3 * d), lambda b, j: (b, 0, 0))]
    in_specs += [_resident(a.shape) for a in operands[2:]]

    return pl.pallas_call(
        _layer_kernel,
        out_shape=jax.ShapeDtypeStruct((bsz, seq, d), x.dtype),
        grid=(bsz, seq // ts),
        in_specs=in_specs,
        out_specs=pl.BlockSpec((None, ts, d), lambda b, j: (b, j, 0)),
        scratch_shapes=[pltpu.VMEM((GLA_HEADS, GLA_HEAD_V, GLA_HEAD_K), F32),
                        pltpu.VMEM((ts + POOL_HALO, d), F32),
                        pltpu.VMEM((ts, GLA_VAL_DIM), F32)],
        compiler_params=pltpu.CompilerParams(dimension_semantics=("arbitrary", "arbitrary"),
                                             vmem_limit_bytes=VMEM_LIMIT_BYTES),
        name="hybrid_pool_gla_layer",
    )(*operands)


@functools.partial(jax.jit, static_argnames=("row_tile",))
def _forward(x, c, g_norm, w_ada, b_ada, w_in, w_pool_group, pool_scale, w_alpha_up, b_alpha, g_gla_head,
             w_pool_out, w_gla_out, w_out, g_final, row_tile=ROW_TILE):
    assert g_norm.shape[0] == 1, "single-layer stack"
    mod = _modulation(c, w_ada[0], b_ada[0])
    return _layer(x, mod, g_norm[0], w_in[0], w_pool_group[0], pool_scale[0], w_alpha_up[0], b_alpha[0],
                  g_gla_head[0], w_pool_out[0], w_gla_out[0], w_out[0], g_final, row_tile)


def kernel(x, c, g_norm, w_ada, b_ada, w_in, w_pool_group, pool_scale, w_alpha_up, b_alpha, g_gla_head,
           w_pool_out, w_gla_out, w_out, g_final):
    return _forward(x, c, g_norm, w_ada, b_ada, w_in, w_pool_group, pool_scale, w_alpha_up, b_alpha,
                    g_gla_head, w_pool_out, w_gla_out, w_out, g_final)
```

```python
import functools

import jax
import jax.numpy as jnp
from jax import lax
from jax.experimental import pallas as pl
from jax.experimental.pallas import tpu as pltpu

F32 = jnp.float32
BF16 = jnp.bfloat16

D_MODEL = 1024
EPS = 1e-6
POOL_WINDOWS = (2, 4, 8, 16)
POOL_GROUP_DIM = D_MODEL // len(POOL_WINDOWS)
POOL_HALO = 16
GLA_HEADS = 4
GLA_KEY_DIM = D_MODEL // 2
GLA_VAL_DIM = D_MODEL
GLA_HEAD_K = GLA_KEY_DIM // GLA_HEADS
GLA_HEAD_V = GLA_VAL_DIM // GLA_HEADS
GLA_GATE_RANK = 16
GLA_GATE_NORMALIZER = 16.0
GLA_BLOCK = 128
LANES = 128
SUBLANES = 8
assert max(POOL_WINDOWS) <= 2 * SUBLANES <= POOL_HALO
ROW_TILE = 512
MOD_COL_TILE = 512
VMEM_LIMIT_BYTES = 60 * 1024 * 1024

_NT = (((1,), (1,)), ((), ()))


def _dot(a, b):
    return jnp.dot(a, b, preferred_element_type=F32)


def _sigmoid(x):
    return 0.5 * jnp.tanh(0.5 * x) + 0.5


def _silu(x):
    half = 0.5 * x
    return half * jnp.tanh(half) + half


def _log_sigmoid(x):
    return jnp.minimum(x, 0.0) - jnp.log1p(jnp.exp(-jnp.abs(x)))


def _mod_kernel(c_ref, w_ref, b_ref, o_ref):
    o_ref[...] = _dot(_silu(c_ref[...]).astype(BF16), w_ref[...].astype(BF16)) + b_ref[...]


def _layer_kernel(x_ref, mod_ref, gnorm_ref, wpv_ref, wpg_ref, wq_ref, wk_ref, wv_ref, wgg_ref, wa_ref,
                  wmgp_ref, wmgg_ref, wgrp_ref, pscale_ref, wup_ref, balpha_ref, ghead_ref, wpo_ref,
                  wgo_ref, wo_ref, gfin_ref, out_ref, st_ref, halo_ref, o_scr, hb_ref):
    j = pl.program_id(1)
    ts = x_ref.shape[0]
    d = D_MODEL

    @pl.when(j == 0)
    def _():
        st_ref[...] = jnp.zeros_like(st_ref)
        halo_ref[...] = jnp.zeros_like(halo_ref)

    x = x_ref[...]
    mod = mod_ref[...]
    shift, scale, gate = mod[:, 0:d], mod[:, d:2 * d], mod[:, 2 * d:3 * d]
    h = x * lax.rsqrt(jnp.mean(x * x, axis=-1, keepdims=True) + EPS) * gnorm_ref[...]
    h = h * (1.0 + scale) + shift
    hb_ref[...] = h.astype(BF16)

    def proj(w_ref):
        return _dot(hb_ref[...], w_ref[...])

    pv = proj(wpv_ref)
    t_plus_1 = lax.broadcasted_iota(jnp.int32, (ts, 1), 0) + (j * ts + 1)
    mixed = []
    for g, w in enumerate(POOL_WINDOWS):
        cols = slice(g * POOL_GROUP_DIM, (g + 1) * POOL_GROUP_DIM)
        win = jnp.concatenate([halo_ref[:, cols], pv[:, cols]], axis=0)
        step = 1
        while step < min(w, SUBLANES):
            win = win + pltpu.roll(win, step, axis=0)
            step *= 2
        if w > SUBLANES:
            win = win[POOL_HALO:, :] + win[POOL_HALO - SUBLANES:-SUBLANES, :]
        else:
            win = win[POOL_HALO:, :]
        inv_cnt = 1.0 / jnp.minimum(t_plus_1, w).astype(F32)
        pooled = win * inv_cnt - pv[:, cols]
        mixed.append(_dot(pooled.astype(BF16), wgrp_ref[g]))
    mixed = jnp.concatenate(mixed, axis=1)
    halo_ref[...] = pv[ts - POOL_HALO:, :]
    y_pool = mixed * pscale_ref[...] * _silu(proj(wpg_ref))
    merged = _sigmoid(proj(wmgp_ref)) * _dot(y_pool.astype(BF16), wpo_ref[...])

    a_low = proj(wa_ref)
    log_a = _log_sigmoid(_dot(a_low.astype(BF16), wup_ref[...]) + balpha_ref[...]) * (1.0 / GLA_GATE_NORMALIZER)
    la_hi = log_a.astype(BF16)
    la_lo = (log_a - la_hi.astype(F32)).astype(BF16)
    q = proj(wq_ref) * (GLA_HEAD_K ** -0.5)
    k = proj(wk_ref)
    v = proj(wv_ref)
    rr = lax.broadcasted_iota(jnp.int32, (GLA_BLOCK, GLA_BLOCK), 0)
    cc = lax.broadcasted_iota(jnp.int32, (GLA_BLOCK, GLA_BLOCK), 1)
    causal = rr >= cc
    tri = jnp.where(causal, 1.0, 0.0).astype(BF16)
    for r in range(ts // GLA_BLOCK):
        rows = slice(r * GLA_BLOCK, (r + 1) * GLA_BLOCK)
        cb = _dot(tri, la_hi[rows, :]) + _dot(tri, la_lo[rows, :])
        b_mid = cb[GLA_BLOCK // 2 - 1:GLA_BLOCK // 2, :]
        b_end = cb[GLA_BLOCK - 1:GLA_BLOCK, :]
        q_mid = q[rows, :] * jnp.exp(cb - b_mid)
        k_mid = k[rows, :] * jnp.exp(b_mid - cb)
        q_in = q_mid.astype(BF16)
        k_in = k_mid.astype(BF16)
        q_start = (q_mid * jnp.exp(b_mid)).astype(BF16)
        k_end = (k_mid * jnp.exp(b_end - b_mid)).astype(BF16)
        block_decay = jnp.exp(b_end)
        for hh in range(GLA_HEADS):
            kc = slice(hh * GLA_HEAD_K, (hh + 1) * GLA_HEAD_K)
            vc = slice(hh * GLA_HEAD_V, (hh + 1) * GLA_HEAD_V)
            scores = lax.dot_general(q_in[:, kc], k_in[:, kc], _NT, preferred_element_type=F32)
            scores = jnp.where(causal, scores, 0.0)
            vb = v[rows, vc]
            state_t = st_ref[hh]
            o = _dot(scores.astype(BF16), vb.astype(BF16))
            o = o + lax.dot_general(q_start[:, kc], state_t.astype(BF16), _NT, preferred_element_type=F32)
            o_scr[rows, vc] = o
            st_ref[hh] = state_t * block_decay[:, kc] + _dot(vb.T.astype(BF16), k_end[:, kc])

    o = o_scr[...]
    normed = []
    for hh in range(GLA_HEADS):
        oh = o[:, hh * GLA_HEAD_V:(hh + 1) * GLA_HEAD_V]
        normed.append(oh * lax.rsqrt(jnp.mean(oh * oh, axis=-1, keepdims=True) + EPS))
    y_gla = jnp.concatenate(normed, axis=1) * ghead_ref[...] * _silu(proj(wgg_ref))
    merged = merged + _sigmoid(proj(wmgg_ref)) * _dot(y_gla.astype(BF16), wgo_ref[...])

    y = x + gate * _dot(merged.astype(BF16), wo_ref[...])
    out_ref[...] = y * lax.rsqrt(jnp.mean(y * y, axis=-1, keepdims=True) + EPS) * gfin_ref[...]


def _resident(shape):
    return pl.BlockSpec(shape, lambda b, j: (0,) * len(shape), pipeline_mode=pl.Buffered(1))


def _modulation(c, w_ada, b_ada):
    bsz, d = c.shape
    n = w_ada.shape[1]
    return pl.pallas_call(
        _mod_kernel,
        out_shape=jax.ShapeDtypeStruct((bsz, n), F32),
        grid=(n // MOD_COL_TILE,),
        in_specs=[pl.BlockSpec((bsz, d), lambda i: (0, 0)),
                  pl.BlockSpec((d, MOD_COL_TILE), lambda i: (0, i)),
                  pl.BlockSpec((1, MOD_COL_TILE), lambda i: (0, i))],
        out_specs=pl.BlockSpec((bsz, MOD_COL_TILE), lambda i: (0, i)),
        name="adaln_modulation",
    )(c, w_ada, b_ada.reshape(1, n))


def _layer(x, mod, g_norm, w_in, w_pool_group, pool_scale, w_alpha_up, b_alpha, g_gla_head, w_pool_out,
           w_gla_out, w_out, g_final, row_tile):
    bsz, seq, d = x.shape
    assert d == D_MODEL and seq % row_tile == 0 and row_tile % GLA_BLOCK == 0
    ts = row_tile

    widths = (d, d, GLA_KEY_DIM, GLA_KEY_DIM, GLA_VAL_DIM, GLA_VAL_DIM, GLA_GATE_RANK, d, d)
    parts, start = [], 0
    for wdt in widths:
        parts.append(w_in[:, start:start + wdt].astype(BF16))
        start += wdt
    wpv, wpg, wq, wk, wv, wgg, wa, wmgp, wmgg = parts
    wa = jnp.pad(wa, ((0, 0), (0, LANES - GLA_GATE_RANK)))
    wup = jnp.pad(w_alpha_up.astype(BF16), ((0, LANES - GLA_GATE_RANK), (0, 0)))

    row = lambda a: a.reshape(1, -1).astype(F32)
    operands = [
        x, mod.reshape(bsz, 1, 3 * d), row(g_norm), wpv, wpg, wq, wk, wv, wgg, wa, wmgp, wmgg,
        w_pool_group.astype(BF16), row(pool_scale), wup, row(b_alpha), row(jnp.tile(g_gla_head, GLA_HEADS)),
        w_pool_out.astype(BF16), w_gla_out.astype(BF16), w_out.astype(BF16), row(g_final),
    ]
    in_specs = [pl.BlockSpec((None, ts, d), lambda b, j: (b, j, 0)),
                pl.BlockSpec((None, 1, 3 * d), lambda b, j: (b, 0, 0))]
    in_specs += [_resident(a.shape) for a in operands[2:]]

    return pl.pallas_call(
        _layer_kernel,
        out_shape=jax.ShapeDtypeStruct((bsz, seq, d), x.dtype),
        grid=(bsz, seq // ts),
        in_specs=in_specs,
        out_specs=pl.BlockSpec((None, ts, d), lambda b, j: (b, j, 0)),
        scratch_shapes=[pltpu.VMEM((GLA_HEADS, GLA_HEAD_V, GLA_HEAD_K), F32),
                        pltpu.VMEM((POOL_HALO, d), F32),
                        pltpu.VMEM((ts, GLA_VAL_DIM), F32),
                        pltpu.VMEM((ts, d), BF16)],
        compiler_params=pltpu.CompilerParams(dimension_semantics=("arbitrary", "arbitrary"),
                                             vmem_limit_bytes=VMEM_LIMIT_BYTES),
        name="hybrid_pool_gla_layer",
    )(*operands)


@functools.partial(jax.jit, static_argnames=("row_tile",))
def _forward(x, c, g_norm, w_ada, b_ada, w_in, w_pool_group, pool_scale, w_alpha_up, b_alpha, g_gla_head,
             w_pool_out, w_gla_out, w_out, g_final, row_tile=ROW_TILE):
    assert g_norm.shape[0] == 1, "single-layer stack"
    mod = _modulation(c, w_ada[0], b_ada[0])
    return _layer(x, mod, g_norm[0], w_in[0], w_pool_group[0], pool_scale[0], w_alpha_up[0], b_alpha[0],
                  g_gla_head[0], w_pool_out[0], w_gla_out[0], w_out[0], g_final, row_tile)


def kernel(x, c, g_norm, w_ada, b_ada, w_in, w_pool_group, pool_scale, w_alpha_up, b_alpha, g_gla_head,
           w_pool_out, w_gla_out, w_out, g_final):
    return _forward(x, c, g_norm, w_ada, b_ada, w_in, w_pool_group, pool_scale, w_alpha_up, b_alpha,
                    g_gla_head, w_pool_out, w_gla_out, w_out, g_final)
```

```python
import functools

import jax
import jax.numpy as jnp
from jax import lax
from jax.experimental import pallas as pl
from jax.experimental.pallas import tpu as pltpu

F32 = jnp.float32
BF16 = jnp.bfloat16

D_MODEL = 1024
EPS = 1e-6
POOL_WINDOWS = (2, 4, 8, 16)
POOL_GROUP_DIM = D_MODEL // len(POOL_WINDOWS)
POOL_HALO = 16
GLA_HEADS = 4
GLA_KEY_DIM = D_MODEL // 2
GLA_VAL_DIM = D_MODEL
GLA_HEAD_K = GLA_KEY_DIM // GLA_HEADS
GLA_HEAD_V = GLA_VAL_DIM // GLA_HEADS
GLA_GATE_RANK = 16
GLA_GATE_NORMALIZER = 16.0
GLA_BLOCK = 128
LANES = 128
SUBLANES = 8
assert max(POOL_WINDOWS) <= 2 * SUBLANES <= POOL_HALO
ROW_TILE = 512
MOD_COL_TILE = 512
VMEM_LIMIT_BYTES = 60 * 1024 * 1024

_IN_SEGMENTS = (("pool_value", D_MODEL), ("pool_gate", D_MODEL), ("q", GLA_KEY_DIM), ("k_gate", GLA_KEY_DIM + LANES),
                ("v", GLA_VAL_DIM), ("gla_gate", GLA_VAL_DIM), ("merge_pool", D_MODEL), ("merge_gla", D_MODEL))
_IN_COLS = {}
_start = 0
for _name, _width in _IN_SEGMENTS:
    _IN_COLS[_name] = (_start, _start + _width)
    _start += _width
IN_WIDTH_PACKED = _start

_NT = (((1,), (1,)), ((), ()))


def _dot(a, b):
    return jnp.dot(a, b, preferred_element_type=F32)


def _sigmoid(x):
    return 0.5 * jnp.tanh(0.5 * x) + 0.5


def _silu(x):
    half = 0.5 * x
    return half * jnp.tanh(half) + half


def _log_sigmoid(x):
    return jnp.minimum(x, 0.0) - jnp.log1p(jnp.exp(-jnp.abs(x)))


def _mod_kernel(c_ref, w_ref, b_ref, o_ref):
    o_ref[...] = _dot(_silu(c_ref[...]).astype(BF16), w_ref[...].astype(BF16)) + b_ref[...]


def _layer_kernel(x_ref, mod_ref, gnorm_ref, win_ref, wgrp_ref, pscale_ref, wup_ref, balpha_ref, ghead_ref, wpo_ref,
                  wgo_ref, wo_ref, gfin_ref, out_ref, st_ref, halo_ref, o_scr, hb_ref):
    j = pl.program_id(1)
    ts = x_ref.shape[0]
    d = D_MODEL

    @pl.when(j == 0)
    def _():
        st_ref[...] = jnp.zeros_like(st_ref)
        halo_ref[...] = jnp.zeros_like(halo_ref)

    x = x_ref[...]
    mod = mod_ref[...]
    shift, scale, gate = mod[:, 0:d], mod[:, d:2 * d], mod[:, 2 * d:3 * d]
    h = x * lax.rsqrt(jnp.mean(x * x, axis=-1, keepdims=True) + EPS) * gnorm_ref[...]
    h = h * (1.0 + scale) + shift
    hb_ref[...] = h.astype(BF16)

    def proj(name, rows=slice(None)):
        c0, c1 = _IN_COLS[name]
        return _dot(hb_ref[rows, :], win_ref[:, c0:c1])

    half = ts // 2
    ka = jnp.concatenate([proj("k_gate", slice(0, half)), proj("k_gate", slice(half, ts))], axis=0)
    k = ka[:, 0:GLA_KEY_DIM]
    a_low = ka[:, GLA_KEY_DIM:GLA_KEY_DIM + LANES]
    log_a = _log_sigmoid(_dot(a_low.astype(BF16), wup_ref[...]) + balpha_ref[...]) * (1.0 / GLA_GATE_NORMALIZER)
    la_hi = log_a.astype(BF16)
    la_lo = (log_a - la_hi.astype(F32)).astype(BF16)
    q = proj("q") * (GLA_HEAD_K ** -0.5)
    v = proj("v")
    rr = lax.broadcasted_iota(jnp.int32, (GLA_BLOCK, GLA_BLOCK), 0)
    cc = lax.broadcasted_iota(jnp.int32, (GLA_BLOCK, GLA_BLOCK), 1)
    causal = rr >= cc
    tri = jnp.where(causal, 1.0, 0.0).astype(BF16)
    nblk = ts // GLA_BLOCK
    cbs = []
    for r in range(nblk):
        rows = slice(r * GLA_BLOCK, (r + 1) * GLA_BLOCK)
        cbs.append(_dot(tri, la_hi[rows, :]) + _dot(tri, la_lo[rows, :]))
    pv = proj("pool_value")

    fillers = ["pool_gate", "merge_pool", "gla_gate", "merge_gla"]
    filled = []
    for r in range(nblk):
        rows = slice(r * GLA_BLOCK, (r + 1) * GLA_BLOCK)
        cb = cbs[r]
        b_mid = cb[GLA_BLOCK // 2 - 1:GLA_BLOCK // 2, :]
        b_end = cb[GLA_BLOCK - 1:GLA_BLOCK, :]
        q_mid = q[rows, :] * jnp.exp(cb - b_mid)
        k_mid = k[rows, :] * jnp.exp(b_mid - cb)
        q_in = q_mid.astype(BF16)
        k_in = k_mid.astype(BF16)
        q_start = (q_mid * jnp.exp(b_mid)).astype(BF16)
        k_end = (k_mid * jnp.exp(b_end - b_mid)).astype(BF16)
        block_decay = jnp.exp(b_end)
        heads = range(GLA_HEADS)
        kcs = [slice(hh * GLA_HEAD_K, (hh + 1) * GLA_HEAD_K) for hh in heads]
        vcs = [slice(hh * GLA_HEAD_V, (hh + 1) * GLA_HEAD_V) for hh in heads]
        scores = [lax.dot_general(q_in[:, kcs[hh]], k_in[:, kcs[hh]], _NT, preferred_element_type=F32) for hh in heads]
        vbs = [v[rows, vcs[hh]] for hh in heads]
        updates = [_dot(vbs[hh].T.astype(BF16), k_end[:, kcs[hh]]) for hh in heads]
        for hh in heads:
            state_t = st_ref[hh]
            o = _dot(jnp.where(causal, scores[hh], 0.0).astype(BF16), vbs[hh].astype(BF16))
            o = o + lax.dot_general(q_start[:, kcs[hh]], state_t.astype(BF16), _NT, preferred_element_type=F32)
            o_scr[rows, vcs[hh]] = o
            st_ref[hh] = state_t * block_decay[:, kcs[hh]] + updates[hh]
        if r < len(fillers):
            filled.append(proj(fillers[r]))
    for name in fillers[len(filled):]:
        filled.append(proj(name))
    pg, mgp, gg, mgg = filled

    t_plus_1 = lax.broadcasted_iota(jnp.int32, (ts, 1), 0) + (j * ts + 1)
    mixed = []
    for g, w in enumerate(POOL_WINDOWS):
        cols = slice(g * POOL_GROUP_DIM, (g + 1) * POOL_GROUP_DIM)
        win = jnp.concatenate([halo_ref[:, cols], pv[:, cols]], axis=0)
        step = 1
        while step < min(w, SUBLANES):
            win = win + pltpu.roll(win, step, axis=0)
            step *= 2
        if w > SUBLANES:
            win = win[POOL_HALO:, :] + win[POOL_HALO - SUBLANES:-SUBLANES, :]
        else:
            win = win[POOL_HALO:, :]
        inv_cnt = 1.0 / jnp.minimum(t_plus_1, w).astype(F32)
        pooled = win * inv_cnt - pv[:, cols]
        mixed.append(_dot(pooled.astype(BF16), wgrp_ref[g]))
    mixed = jnp.concatenate(mixed, axis=1)
    halo_ref[...] = pv[ts - POOL_HALO:, :]
    y_pool = mixed * pscale_ref[...] * _silu(pg)
    merged = _sigmoid(mgp) * _dot(y_pool.astype(BF16), wpo_ref[...])

    o = o_scr[...]
    normed = []
    for hh in range(GLA_HEADS):
        oh = o[:, hh * GLA_HEAD_V:(hh + 1) * GLA_HEAD_V]
        normed.append(oh * lax.rsqrt(jnp.mean(oh * oh, axis=-1, keepdims=True) + EPS))
    y_gla = (jnp.concatenate(normed, axis=1) * ghead_ref[...] * _silu(gg)).astype(BF16)
    gate_gla = _sigmoid(mgg)
    gfin = gfin_ref[...]
    halves = [slice(0, half), slice(half, ts)]
    g_out = [_dot(y_gla[hs, :], wgo_ref[...]) for hs in halves]
    for hs, g_half in zip(halves, g_out):
        merged_half = merged[hs, :] + gate_gla[hs, :] * g_half
        y = x_ref[hs, :] + gate * _dot(merged_half.astype(BF16), wo_ref[...])
        out_ref[hs, :] = y * lax.rsqrt(jnp.mean(y * y, axis=-1, keepdims=True) + EPS) * gfin


def _resident(shape):
    return pl.BlockSpec(shape, lambda b, j: (0,) * len(shape), pipeline_mode=pl.Buffered(1))


def _modulation(c, w_ada, b_ada):
    bsz, d = c.shape
    n = w_ada.shape[1]
    return pl.pallas_call(
        _mod_kernel,
        out_shape=jax.ShapeDtypeStruct((bsz, n), F32),
        grid=(n // MOD_COL_TILE,),
        in_specs=[pl.BlockSpec((bsz, d), lambda i: (0, 0)),
                  pl.BlockSpec((d, MOD_COL_TILE), lambda i: (0, i)),
                  pl.BlockSpec((1, MOD_COL_TILE), lambda i: (0, i))],
        out_specs=pl.BlockSpec((bsz, MOD_COL_TILE), lambda i: (0, i)),
        name="adaln_modulation",
    )(c, w_ada, b_ada.reshape(1, n))


def _layer(x, mod, g_norm, w_in, w_pool_group, pool_scale, w_alpha_up, b_alpha, g_gla_head, w_pool_out,
           w_gla_out, w_out, g_final, row_tile):
    bsz, seq, d = x.shape
    assert d == D_MODEL and seq % row_tile == 0 and row_tile % (2 * GLA_BLOCK) == 0
    ts = row_tile

    ref_widths = (d, d, GLA_KEY_DIM, GLA_KEY_DIM, GLA_VAL_DIM, GLA_VAL_DIM, GLA_GATE_RANK, d, d)
    bounds = [0]
    for wdt in ref_widths:
        bounds.append(bounds[-1] + wdt)
    seg = lambda i: w_in[:, bounds[i]:bounds[i + 1]]
    gate_pad = jnp.zeros((d, LANES - GLA_GATE_RANK), w_in.dtype)
    w_packed = jnp.concatenate([seg(0), seg(1), seg(2), seg(3), seg(6), gate_pad, seg(4), seg(5), seg(7), seg(8)],
                               axis=1).astype(BF16)
    assert w_packed.shape == (d, IN_WIDTH_PACKED)
    wup = jnp.pad(w_alpha_up.astype(BF16), ((0, LANES - GLA_GATE_RANK), (0, 0)))

    row = lambda a: a.reshape(1, -1).astype(F32)
    operands = [
        x, mod.reshape(bsz, 1, 3 * d), row(g_norm), w_packed,
        w_pool_group.astype(BF16), row(pool_scale), wup, row(b_alpha), row(jnp.tile(g_gla_head, GLA_HEADS)),
        w_pool_out.astype(BF16), w_gla_out.astype(BF16), w_out.astype(BF16), row(g_final),
    ]
    in_specs = [pl.BlockSpec((None, ts, d), lambda b, j: (b, j, 0)),
                pl.BlockSpec((None, 1, 3 * d), lambda b, j: (b, 0, 0))]
    in_specs += [_resident(a.shape) for a in operands[2:]]

    return pl.pallas_call(
        _layer_kernel,
        out_shape=jax.ShapeDtypeStruct((bsz, seq, d), x.dtype),
        grid=(bsz, seq // ts),
        in_specs=in_specs,
        out_specs=pl.BlockSpec((None, ts, d), lambda b, j: (b, j, 0)),
        scratch_shapes=[pltpu.VMEM((GLA_HEADS, GLA_HEAD_V, GLA_HEAD_K), F32),
                        pltpu.VMEM((POOL_HALO, d), F32),
                        pltpu.VMEM((ts, GLA_VAL_DIM), F32),
                        pltpu.VMEM((ts, d), BF16)],
        compiler_params=pltpu.CompilerParams(dimension_semantics=("arbitrary", "arbitrary"),
                                             vmem_limit_bytes=VMEM_LIMIT_BYTES),
        name="hybrid_pool_gla_layer",
    )(*operands)


@functools.partial(jax.jit, static_argnames=("row_tile",))
def _forward(x, c, g_norm, w_ada, b_ada, w_in, w_pool_group, pool_scale, w_alpha_up, b_alpha, g_gla_head,
             w_pool_out, w_gla_out, w_out, g_final, row_tile=ROW_TILE):
    assert g_norm.shape[0] == 1, "single-layer stack"
    mod = _modulation(c, w_ada[0], b_ada[0])
    return _layer(x, mod, g_norm[0], w_in[0], w_pool_group[0], pool_scale[0], w_alpha_up[0], b_alpha[0],
                  g_gla_head[0], w_pool_out[0], w_gla_out[0], w_out[0], g_final, row_tile)


def kernel(x, c, g_norm, w_ada, b_ada, w_in, w_pool_group, pool_scale, w_alpha_up, b_alpha, g_gla_head,
           w_pool_out, w_gla_out, w_out, g_final):
    return _forward(x, c, g_norm, w_ada, b_ada, w_in, w_pool_group, pool_scale, w_alpha_up, b_alpha,
                    g_gla_head, w_pool_out, w_gla_out, w_out, g_final)
```

```python
import functools

import jax
import jax.numpy as jnp
from jax import lax
from jax.experimental import pallas as pl
from jax.experimental.pallas import tpu as pltpu

F32 = jnp.float32
BF16 = jnp.bfloat16

D_MODEL = 1024
EPS = 1e-6
POOL_WINDOWS = (2, 4, 8, 16)
POOL_GROUP_DIM = D_MODEL // len(POOL_WINDOWS)
POOL_HALO = 16
GLA_HEADS = 4
GLA_KEY_DIM = D_MODEL // 2
GLA_VAL_DIM = D_MODEL
GLA_HEAD_K = GLA_KEY_DIM // GLA_HEADS
GLA_HEAD_V = GLA_VAL_DIM // GLA_HEADS
GLA_GATE_RANK = 16
GLA_GATE_NORMALIZER = 16.0
GLA_BLOCK = 128
LANES = 128
SUBLANES = 8
assert max(POOL_WINDOWS) <= 2 * SUBLANES <= POOL_HALO
ROW_TILE = 512
MOD_COL_TILE = 512
VMEM_LIMIT_BYTES = 60 * 1024 * 1024

_IN_SEGMENTS = (("pool_value", D_MODEL), ("pool_gate", D_MODEL), ("q", GLA_KEY_DIM), ("k_gate", GLA_KEY_DIM + LANES),
                ("v", GLA_VAL_DIM), ("gla_gate", GLA_VAL_DIM), ("merge_pool", D_MODEL), ("merge_gla", D_MODEL))
_IN_COLS = {}
_start = 0
for _name, _width in _IN_SEGMENTS:
    _IN_COLS[_name] = (_start, _start + _width)
    _start += _width
IN_WIDTH_PACKED = _start
_REF_COLS = {"pool_value": 0, "pool_gate": D_MODEL, "q": 2 * D_MODEL, "k_gate": 2 * D_MODEL + GLA_KEY_DIM,
             "v": 2 * D_MODEL + 2 * GLA_KEY_DIM, "gla_gate": 2 * D_MODEL + 2 * GLA_KEY_DIM + GLA_VAL_DIM}
_REF_GATE_COL = _REF_COLS["gla_gate"] + GLA_VAL_DIM
_REF_COLS["merge_pool"] = _REF_GATE_COL + GLA_GATE_RANK
_REF_COLS["merge_gla"] = _REF_COLS["merge_pool"] + D_MODEL
IN_WIDTH_REF = _REF_COLS["merge_gla"] + D_MODEL
PACK_ROW_TILE = 128

_NT = (((1,), (1,)), ((), ()))


def _dot(a, b):
    return jnp.dot(a, b, preferred_element_type=F32)


def _sigmoid(x):
    return 0.5 * jnp.tanh(0.5 * x) + 0.5


def _silu(x):
    half = 0.5 * x
    return half * jnp.tanh(half) + half


def _log_sigmoid(x):
    return jnp.minimum(x, 0.0) - jnp.log1p(jnp.exp(-jnp.abs(x)))


def _mod_kernel(c_ref, w_ref, b_ref, o_ref):
    o_ref[...] = _dot(_silu(c_ref[...]).astype(BF16), w_ref[...].astype(BF16)) + b_ref[...]


def _pack_kernel(w_ref, o_ref):
    w = w_ref[...]
    for name, src in _REF_COLS.items():
        c0, c1 = _IN_COLS[name]
        if name == "k_gate":
            o_ref[:, c0:c0 + GLA_KEY_DIM] = w[:, src:src + GLA_KEY_DIM].astype(BF16)
            g0 = _REF_GATE_COL
            tile = w[:, g0:g0 + LANES]
            lane = lax.broadcasted_iota(jnp.int32, tile.shape, 1)
            o_ref[:, c0 + GLA_KEY_DIM:c1] = jnp.where(lane < GLA_GATE_RANK, tile, 0.0).astype(BF16)
        else:
            o_ref[:, c0:c1] = w[:, src:src + (c1 - c0)].astype(BF16)


def _layer_kernel(x_ref, mod_ref, gnorm_ref, win_ref, wgrp_ref, pscale_ref, wup_ref, balpha_ref, ghead_ref, wpo_ref,
                  wgo_ref, wo_ref, gfin_ref, out_ref, st_ref, halo_ref, o_scr, hb_ref):
    j = pl.program_id(1)
    ts = x_ref.shape[0]
    d = D_MODEL

    @pl.when(j == 0)
    def _():
        st_ref[...] = jnp.zeros_like(st_ref)
        halo_ref[...] = jnp.zeros_like(halo_ref)

    x = x_ref[...]
    mod = mod_ref[...]
    shift, scale, gate = mod[:, 0:d], mod[:, d:2 * d], mod[:, 2 * d:3 * d]
    h = x * lax.rsqrt(jnp.mean(x * x, axis=-1, keepdims=True) + EPS) * gnorm_ref[...]
    h = h * (1.0 + scale) + shift
    hb_ref[...] = h.astype(BF16)

    def proj(name, rows=slice(None)):
        c0, c1 = _IN_COLS[name]
        return _dot(hb_ref[rows, :], win_ref[:, c0:c1])

    half = ts // 2
    ka = jnp.concatenate([proj("k_gate", slice(0, half)), proj("k_gate", slice(half, ts))], axis=0)
    k = ka[:, 0:GLA_KEY_DIM]
    a_low = ka[:, GLA_KEY_DIM:GLA_KEY_DIM + LANES]
    log_a = _log_sigmoid(_dot(a_low.astype(BF16), wup_ref[...]) + balpha_ref[...]) * (1.0 / GLA_GATE_NORMALIZER)
    la_hi = log_a.astype(BF16)
    la_lo = (log_a - la_hi.astype(F32)).astype(BF16)
    q = proj("q") * (GLA_HEAD_K ** -0.5)
    v = proj("v")
    rr = lax.broadcasted_iota(jnp.int32, (GLA_BLOCK, GLA_BLOCK), 0)
    cc = lax.broadcasted_iota(jnp.int32, (GLA_BLOCK, GLA_BLOCK), 1)
    causal = rr >= cc
    tri = jnp.where(causal, 1.0, 0.0).astype(BF16)
    nblk = ts // GLA_BLOCK
    cbs = []
    for r in range(nblk):
        rows = slice(r * GLA_BLOCK, (r + 1) * GLA_BLOCK)
        cbs.append(_dot(tri, la_hi[rows, :]) + _dot(tri, la_lo[rows, :]))
    pv = proj("pool_value")

    fillers = ["pool_gate", "merge_pool", "gla_gate", "merge_gla"]
    filled = []
    for r in range(nblk):
        rows = slice(r * GLA_BLOCK, (r + 1) * GLA_BLOCK)
        cb = cbs[r]
        b_mid = cb[GLA_BLOCK // 2 - 1:GLA_BLOCK // 2, :]
        b_end = cb[GLA_BLOCK - 1:GLA_BLOCK, :]
        q_mid = q[rows, :] * jnp.exp(cb - b_mid)
        k_mid = k[rows, :] * jnp.exp(b_mid - cb)
        q_in = q_mid.astype(BF16)
        k_in = k_mid.astype(BF16)
        q_start = (q_mid * jnp.exp(b_mid)).astype(BF16)
        k_end = (k_mid * jnp.exp(b_end - b_mid)).astype(BF16)
        block_decay = jnp.exp(b_end)
        heads = range(GLA_HEADS)
        kcs = [slice(hh * GLA_HEAD_K, (hh + 1) * GLA_HEAD_K) for hh in heads]
        vcs = [slice(hh * GLA_HEAD_V, (hh + 1) * GLA_HEAD_V) for hh in heads]
        scores = [lax.dot_general(q_in[:, kcs[hh]], k_in[:, kcs[hh]], _NT, preferred_element_type=F32) for hh in heads]
        vbs = [v[rows, vcs[hh]] for hh in heads]
        updates = [_dot(vbs[hh].T.astype(BF16), k_end[:, kcs[hh]]) for hh in heads]
        for hh in heads:
            state_t = st_ref[hh]
            o = _dot(jnp.where(causal, scores[hh], 0.0).astype(BF16), vbs[hh].astype(BF16))
            o = o + lax.dot_general(q_start[:, kcs[hh]], state_t.astype(BF16), _NT, preferred_element_type=F32)
            o_scr[rows, vcs[hh]] = o
            st_ref[hh] = state_t * block_decay[:, kcs[hh]] + updates[hh]
        if r < len(fillers):
            filled.append(proj(fillers[r]))
    for name in fillers[len(filled):]:
        filled.append(proj(name))
    pg, mgp, gg, mgg = filled

    t_plus_1 = lax.broadcasted_iota(jnp.int32, (ts, 1), 0) + (j * ts + 1)
    mixed = []
    for g, w in enumerate(POOL_WINDOWS):
        cols = slice(g * POOL_GROUP_DIM, (g + 1) * POOL_GROUP_DIM)
        win = jnp.concatenate([halo_ref[:, cols], pv[:, cols]], axis=0)
        step = 1
        while step < min(w, SUBLANES):
            win = win + pltpu.roll(win, step, axis=0)
            step *= 2
        if w > SUBLANES:
            win = win[POOL_HALO:, :] + win[POOL_HALO - SUBLANES:-SUBLANES, :]
        else:
            win = win[POOL_HALO:, :]
        inv_cnt = 1.0 / jnp.minimum(t_plus_1, w).astype(F32)
        pooled = win * inv_cnt - pv[:, cols]
        mixed.append(_dot(pooled.astype(BF16), wgrp_ref[g]))
    mixed = jnp.concatenate(mixed, axis=1)
    halo_ref[...] = pv[ts - POOL_HALO:, :]
    y_pool = mixed * pscale_ref[...] * _silu(pg)
    merged = _sigmoid(mgp) * _dot(y_pool.astype(BF16), wpo_ref[...])

    o = o_scr[...]
    normed = []
    for hh in range(GLA_HEADS):
        oh = o[:, hh * GLA_HEAD_V:(hh + 1) * GLA_HEAD_V]
        normed.append(oh * lax.rsqrt(jnp.mean(oh * oh, axis=-1, keepdims=True) + EPS))
    y_gla = (jnp.concatenate(normed, axis=1) * ghead_ref[...] * _silu(gg)).astype(BF16)
    gate_gla = _sigmoid(mgg)
    gfin = gfin_ref[...]
    halves = [slice(0, half), slice(half, ts)]
    g_out = [_dot(y_gla[hs, :], wgo_ref[...]) for hs in halves]
    for hs, g_half in zip(halves, g_out):
        merged_half = merged[hs, :] + gate_gla[hs, :] * g_half
        y = x_ref[hs, :] + gate * _dot(merged_half.astype(BF16), wo_ref[...])
        out_ref[hs, :] = y * lax.rsqrt(jnp.mean(y * y, axis=-1, keepdims=True) + EPS) * gfin


def _resident(shape):
    return pl.BlockSpec(shape, lambda b, j: (0,) * len(shape), pipeline_mode=pl.Buffered(1))


def _modulation(c, w_ada, b_ada):
    bsz, d = c.shape
    n = w_ada.shape[1]
    return pl.pallas_call(
        _mod_kernel,
        out_shape=jax.ShapeDtypeStruct((bsz, n), F32),
        grid=(n // MOD_COL_TILE,),
        in_specs=[pl.BlockSpec((bsz, d), lambda i: (0, 0)),
                  pl.BlockSpec((d, MOD_COL_TILE), lambda i: (0, i)),
                  pl.BlockSpec((1, MOD_COL_TILE), lambda i: (0, i))],
        out_specs=pl.BlockSpec((bsz, MOD_COL_TILE), lambda i: (0, i)),
        name="adaln_modulation",
    )(c, w_ada, b_ada.reshape(1, n))


def _layer(x, mod, g_norm, w_in, w_pool_group, pool_scale, w_alpha_up, b_alpha, g_gla_head, w_pool_out,
           w_gla_out, w_out, g_final, row_tile):
    bsz, seq, d = x.shape
    assert d == D_MODEL and seq % row_tile == 0 and row_tile % (2 * GLA_BLOCK) == 0
    ts = row_tile

    assert w_in.shape == (d, IN_WIDTH_REF)
    w_packed = pl.pallas_call(
        _pack_kernel,
        out_shape=jax.ShapeDtypeStruct((d, IN_WIDTH_PACKED), BF16),
        grid=(d // PACK_ROW_TILE,),
        in_specs=[pl.BlockSpec((PACK_ROW_TILE, IN_WIDTH_REF), lambda i: (i, 0))],
        out_specs=pl.BlockSpec((PACK_ROW_TILE, IN_WIDTH_PACKED), lambda i: (i, 0)),
        name="pack_input_projection",
    )(w_in)
    wup = jnp.pad(w_alpha_up.astype(BF16), ((0, LANES - GLA_GATE_RANK), (0, 0)))

    row = lambda a: a.reshape(1, -1).astype(F32)
    operands = [
        x, mod.reshape(bsz, 1, 3 * d), row(g_norm), w_packed,
        w_pool_group.astype(BF16), row(pool_scale), wup, row(b_alpha), row(jnp.tile(g_gla_head, GLA_HEADS)),
        w_pool_out.astype(BF16), w_gla_out.astype(BF16), w_out.astype(BF16), row(g_final),
    ]
    in_specs = [pl.BlockSpec((None, ts, d), lambda b, j: (b, j, 0)),
                pl.BlockSpec((None, 1, 3 * d), lambda b, j: (b, 0, 0))]
    in_specs += [_resident(a.shape) for a in operands[2:]]

    return pl.pallas_call(
        _layer_kernel,
        out_shape=jax.ShapeDtypeStruct((bsz, seq, d), x.dtype),
        grid=(bsz, seq // ts),
        in_specs=in_specs,
        out_specs=pl.BlockSpec((None, ts, d), lambda b, j: (b, j, 0)),
        scratch_shapes=[pltpu.VMEM((GLA_HEADS, GLA_HEAD_V, GLA_HEAD_K), F32),
                        pltpu.VMEM((POOL_HALO, d), F32),
                        pltpu.VMEM((ts, GLA_VAL_DIM), F32),
                        pltpu.VMEM((ts, d), BF16)],
        compiler_params=pltpu.CompilerParams(dimension_semantics=("arbitrary", "arbitrary"),
                                             vmem_limit_bytes=VMEM_LIMIT_BYTES),
        name="hybrid_pool_gla_layer",
    )(*operands)


@functools.partial(jax.jit, static_argnames=("row_tile",))
def _forward(x, c, g_norm, w_ada, b_ada, w_in, w_pool_group, pool_scale, w_alpha_up, b_alpha, g_gla_head,
             w_pool_out, w_gla_out, w_out, g_final, row_tile=ROW_TILE):
    assert g_norm.shape[0] == 1, "single-layer stack"
    mod = _modulation(c, w_ada[0], b_ada[0])
    return _layer(x, mod, g_norm[0], w_in[0], w_pool_group[0], pool_scale[0], w_alpha_up[0], b_alpha[0],
                  g_gla_head[0], w_pool_out[0], w_gla_out[0], w_out[0], g_final, row_tile)


def kernel(x, c, g_norm, w_ada, b_ada, w_in, w_pool_group, pool_scale, w_alpha_up, b_alpha, g_gla_head,
           w_pool_out, w_gla_out, w_out, g_final):
    return _forward(x, c, g_norm, w_ada, b_ada, w_in, w_pool_group, pool_scale, w_alpha_up, b_alpha,
                    g_gla_head, w_pool_out, w_gla_out, w_out, g_final)
```

```python
import functools

import jax
import jax.numpy as jnp
from jax import lax
from jax.experimental import pallas as pl
from jax.experimental.pallas import tpu as pltpu

F32 = jnp.float32
BF16 = jnp.bfloat16

D_MODEL = 1024
EPS = 1e-6
POOL_WINDOWS = (2, 4, 8, 16)
POOL_GROUP_DIM = D_MODEL // len(POOL_WINDOWS)
POOL_HALO = 16
GLA_HEADS = 4
GLA_KEY_DIM = D_MODEL // 2
GLA_VAL_DIM = D_MODEL
GLA_HEAD_K = GLA_KEY_DIM // GLA_HEADS
GLA_HEAD_V = GLA_VAL_DIM // GLA_HEADS
GLA_GATE_RANK = 16
GLA_GATE_NORMALIZER = 16.0
GLA_BLOCK = 128
LANES = 128
SUBLANES = 8
assert max(POOL_WINDOWS) <= 2 * SUBLANES <= POOL_HALO
ROW_TILE = 512
MOD_COL_TILE = 512
VMEM_LIMIT_BYTES = 60 * 1024 * 1024

_IN_SEGMENTS = (("pool_value", D_MODEL), ("pool_gate", D_MODEL), ("q", GLA_KEY_DIM), ("k", GLA_KEY_DIM),
                ("v", GLA_VAL_DIM), ("gla_gate", GLA_VAL_DIM), ("merge_pool", D_MODEL), ("merge_gla", D_MODEL),
                ("decay_gate", LANES))
_IN_COLS = {}
_start = 0
for _name, _width in _IN_SEGMENTS:
    _IN_COLS[_name] = (_start, _start + _width)
    _start += _width
IN_WIDTH_PACKED = _start
_REF_GATE_COL = 2 * D_MODEL + 2 * GLA_KEY_DIM + 2 * GLA_VAL_DIM
IN_WIDTH_REF = _REF_GATE_COL + GLA_GATE_RANK + 2 * D_MODEL
PACK_TILE = 1024
_PACK_SRC_COLS = (0, 1024, 2048, 3072, 4096, _REF_GATE_COL + GLA_GATE_RANK, _REF_GATE_COL + GLA_GATE_RANK + D_MODEL,
                  _REF_GATE_COL)
assert (len(_PACK_SRC_COLS) - 1) * PACK_TILE + LANES == IN_WIDTH_PACKED

_NT = (((1,), (1,)), ((), ()))


def _dot(a, b):
    return jnp.dot(a, b, preferred_element_type=F32)


def _sigmoid(x):
    return 0.5 * jnp.tanh(0.5 * x) + 0.5


def _silu(x):
    half = 0.5 * x
    return half * jnp.tanh(half) + half


def _log_sigmoid(x):
    return jnp.minimum(x, 0.0) - jnp.log1p(jnp.exp(-jnp.abs(x)))


def _mod_kernel(c_ref, w_ref, b_ref, o_ref):
    o_ref[...] = _dot(_silu(c_ref[...]).astype(BF16), w_ref[...].astype(BF16)) + b_ref[...]


def _pack_kernel(src_ref, wt_ref, o_ref):
    i = pl.program_id(0)
    w = wt_ref[...].T
    lane = lax.broadcasted_iota(jnp.int32, w.shape, 1)
    is_gate_tile = i == len(_PACK_SRC_COLS) - 1
    o_ref[...] = jnp.where(jnp.logical_and(is_gate_tile, lane >= GLA_GATE_RANK), 0.0, w).astype(BF16)


def _layer_kernel(x_ref, mod_ref, gnorm_ref, win_ref, wgrp_ref, pscale_ref, wup_ref, balpha_ref, ghead_ref, wpo_ref,
                  wgo_ref, wo_ref, gfin_ref, out_ref, st_ref, halo_ref, o_scr, hb_ref):
    j = pl.program_id(1)
    ts = x_ref.shape[0]
    d = D_MODEL

    @pl.when(j == 0)
    def _():
        st_ref[...] = jnp.zeros_like(st_ref)
        halo_ref[...] = jnp.zeros_like(halo_ref)

    x = x_ref[...]
    mod = mod_ref[...]
    shift, scale, gate = mod[:, 0:d], mod[:, d:2 * d], mod[:, 2 * d:3 * d]
    h = x * lax.rsqrt(jnp.mean(x * x, axis=-1, keepdims=True) + EPS) * gnorm_ref[...]
    h = h * (1.0 + scale) + shift
    hb_ref[...] = h.astype(BF16)

    def proj(name, rows=slice(None)):
        c0, c1 = _IN_COLS[name]
        return _dot(hb_ref[rows, :], win_ref[:, c0:c1])

    half = ts // 2
    k = jnp.concatenate([proj("k", slice(0, half)), proj("k", slice(half, ts))], axis=0)
    a_low = proj("decay_gate")
    log_a = _log_sigmoid(_dot(a_low.astype(BF16), wup_ref[...]) + balpha_ref[...]) * (1.0 / GLA_GATE_NORMALIZER)
    la_hi = log_a.astype(BF16)
    la_lo = (log_a - la_hi.astype(F32)).astype(BF16)
    q = proj("q") * (GLA_HEAD_K ** -0.5)
    v = proj("v")
    rr = lax.broadcasted_iota(jnp.int32, (GLA_BLOCK, GLA_BLOCK), 0)
    cc = lax.broadcasted_iota(jnp.int32, (GLA_BLOCK, GLA_BLOCK), 1)
    causal = rr >= cc
    tri = jnp.where(causal, 1.0, 0.0).astype(BF16)
    nblk = ts // GLA_BLOCK
    cbs = []
    for r in range(nblk):
        rows = slice(r * GLA_BLOCK, (r + 1) * GLA_BLOCK)
        cbs.append(_dot(tri, la_hi[rows, :]) + _dot(tri, la_lo[rows, :]))
    pv = proj("pool_value")

    fillers = ["pool_gate", "merge_pool", "gla_gate", "merge_gla"]
    filled = []
    for r in range(nblk):
        rows = slice(r * GLA_BLOCK, (r + 1) * GLA_BLOCK)
        cb = cbs[r]
        b_mid = cb[GLA_BLOCK // 2 - 1:GLA_BLOCK // 2, :]
        b_end = cb[GLA_BLOCK - 1:GLA_BLOCK, :]
        q_mid = q[rows, :] * jnp.exp(cb - b_mid)
        k_mid = k[rows, :] * jnp.exp(b_mid - cb)
        q_in = q_mid.astype(BF16)
        k_in = k_mid.astype(BF16)
        q_start = (q_mid * jnp.exp(b_mid)).astype(BF16)
        k_end = (k_mid * jnp.exp(b_end - b_mid)).astype(BF16)
        block_decay = jnp.exp(b_end)
        heads = range(GLA_HEADS)
        kcs = [slice(hh * GLA_HEAD_K, (hh + 1) * GLA_HEAD_K) for hh in heads]
        vcs = [slice(hh * GLA_HEAD_V, (hh + 1) * GLA_HEAD_V) for hh in heads]
        scores = [lax.dot_general(q_in[:, kcs[hh]], k_in[:, kcs[hh]], _NT, preferred_element_type=F32) for hh in heads]
        vbs = [v[rows, vcs[hh]] for hh in heads]
        updates = [_dot(vbs[hh].T.astype(BF16), k_end[:, kcs[hh]]) for hh in heads]
        for hh in heads:
            state_t = st_ref[hh]
            o = _dot(jnp.where(causal, scores[hh], 0.0).astype(BF16), vbs[hh].astype(BF16))
            o = o + lax.dot_general(q_start[:, kcs[hh]], state_t.astype(BF16), _NT, preferred_element_type=F32)
            o_scr[rows, vcs[hh]] = o
            st_ref[hh] = state_t * block_decay[:, kcs[hh]] + updates[hh]
        if r < len(fillers):
            filled.append(proj(fillers[r]))
    for name in fillers[len(filled):]:
        filled.append(proj(name))
    pg, mgp, gg, mgg = filled

    t_plus_1 = lax.broadcasted_iota(jnp.int32, (ts, 1), 0) + (j * ts + 1)
    mixed = []
    for g, w in enumerate(POOL_WINDOWS):
        cols = slice(g * POOL_GROUP_DIM, (g + 1) * POOL_GROUP_DIM)
        win = jnp.concatenate([halo_ref[:, cols], pv[:, cols]], axis=0)
        step = 1
        while step < min(w, SUBLANES):
            win = win + pltpu.roll(win, step, axis=0)
            step *= 2
        if w > SUBLANES:
            win = win[POOL_HALO:, :] + win[POOL_HALO - SUBLANES:-SUBLANES, :]
        else:
            win = win[POOL_HALO:, :]
        inv_cnt = 1.0 / jnp.minimum(t_plus_1, w).astype(F32)
        pooled = win * inv_cnt - pv[:, cols]
        mixed.append(_dot(pooled.astype(BF16), wgrp_ref[g]))
    mixed = jnp.concatenate(mixed, axis=1)
    halo_ref[...] = pv[ts - POOL_HALO:, :]
    y_pool = mixed * pscale_ref[...] * _silu(pg)
    merged = _sigmoid(mgp) * _dot(y_pool.astype(BF16), wpo_ref[...])

    o = o_scr[...]
    normed = []
    for hh in range(GLA_HEADS):
        oh = o[:, hh * GLA_HEAD_V:(hh + 1) * GLA_HEAD_V]
        normed.append(oh * lax.rsqrt(jnp.mean(oh * oh, axis=-1, keepdims=True) + EPS))
    y_gla = (jnp.concatenate(normed, axis=1) * ghead_ref[...] * _silu(gg)).astype(BF16)
    gate_gla = _sigmoid(mgg)
    gfin = gfin_ref[...]
    halves = [slice(0, half), slice(half, ts)]
    g_out = [_dot(y_gla[hs, :], wgo_ref[...]) for hs in halves]
    for hs, g_half in zip(halves, g_out):
        merged_half = merged[hs, :] + gate_gla[hs, :] * g_half
        y = x_ref[hs, :] + gate * _dot(merged_half.astype(BF16), wo_ref[...])
        out_ref[hs, :] = y * lax.rsqrt(jnp.mean(y * y, axis=-1, keepdims=True) + EPS) * gfin


def _resident(shape):
    return pl.BlockSpec(shape, lambda b, j: (0,) * len(shape), pipeline_mode=pl.Buffered(1))


def _modulation(c, w_ada, b_ada):
    bsz, d = c.shape
    n = w_ada.shape[1]
    return pl.pallas_call(
        _mod_kernel,
        out_shape=jax.ShapeDtypeStruct((bsz, n), F32),
        grid=(n // MOD_COL_TILE,),
        in_specs=[pl.BlockSpec((bsz, d), lambda i: (0, 0)),
                  pl.BlockSpec((d, MOD_COL_TILE), lambda i: (0, i)),
                  pl.BlockSpec((1, MOD_COL_TILE), lambda i: (0, i))],
        out_specs=pl.BlockSpec((bsz, MOD_COL_TILE), lambda i: (0, i)),
        name="adaln_modulation",
    )(c, w_ada, b_ada.reshape(1, n))


def _layer(x, mod, g_norm, w_in, w_pool_group, pool_scale, w_alpha_up, b_alpha, g_gla_head, w_pool_out,
           w_gla_out, w_out, g_final, row_tile):
    bsz, seq, d = x.shape
    assert d == D_MODEL and seq % row_tile == 0 and row_tile % (2 * GLA_BLOCK) == 0
    ts = row_tile

    assert w_in.shape == (d, IN_WIDTH_REF)
    assert all(c % SUBLANES == 0 for c in _PACK_SRC_COLS)
    src_cols = jnp.asarray([c // SUBLANES for c in _PACK_SRC_COLS], jnp.int32)
    w_packed = pl.pallas_call(
        _pack_kernel,
        out_shape=jax.ShapeDtypeStruct((d, IN_WIDTH_PACKED), BF16),
        grid_spec=pltpu.PrefetchScalarGridSpec(
            num_scalar_prefetch=1,
            grid=(len(_PACK_SRC_COLS),),
            in_specs=[pl.BlockSpec((pl.Element(PACK_TILE), pl.Element(d)), lambda i, src: (src[i] * SUBLANES, 0))],
            out_specs=pl.BlockSpec((d, PACK_TILE), lambda i, src: (0, i)),
        ),
        name="pack_input_projection",
    )(src_cols, w_in.T)
    wup = jnp.pad(w_alpha_up.astype(BF16), ((0, LANES - GLA_GATE_RANK), (0, 0)))

    row = lambda a: a.reshape(1, -1).astype(F32)
    operands = [
        x, mod.reshape(bsz, 1, 3 * d), row(g_norm), w_packed,
        w_pool_group.astype(BF16), row(pool_scale), wup, row(b_alpha), row(jnp.tile(g_gla_head, GLA_HEADS)),
        w_pool_out.astype(BF16), w_gla_out.astype(BF16), w_out.astype(BF16), row(g_final),
    ]
    in_specs = [pl.BlockSpec((None, ts, d), lambda b, j: (b, j, 0)),
                pl.BlockSpec((None, 1, 3 * d), lambda b, j: (b, 0, 0))]
    in_specs += [_resident(a.shape) for a in operands[2:]]

    return pl.pallas_call(
        _layer_kernel,
        out_shape=jax.ShapeDtypeStruct((bsz, seq, d), x.dtype),
        grid=(bsz, seq // ts),
        in_specs=in_specs,
        out_specs=pl.BlockSpec((None, ts, d), lambda b, j: (b, j, 0)),
        scratch_shapes=[pltpu.VMEM((GLA_HEADS, GLA_HEAD_V, GLA_HEAD_K), F32),
                        pltpu.VMEM((POOL_HALO, d), F32),
                        pltpu.VMEM((ts, GLA_VAL_DIM), F32),
                        pltpu.VMEM((ts, d), BF16)],
        compiler_params=pltpu.CompilerParams(dimension_semantics=("arbitrary", "arbitrary"),
                                             vmem_limit_bytes=VMEM_LIMIT_BYTES),
        name="hybrid_pool_gla_layer",
    )(*operands)


@functools.partial(jax.jit, static_argnames=("row_tile",))
def _forward(x, c, g_norm, w_ada, b_ada, w_in, w_pool_group, pool_scale, w_alpha_up, b_alpha, g_gla_head,
             w_pool_out, w_gla_out, w_out, g_final, row_tile=ROW_TILE):
    assert g_norm.shape[0] == 1, "single-layer stack"
    mod = _modulation(c, w_ada[0], b_ada[0])
    return _layer(x, mod, g_norm[0], w_in[0], w_pool_group[0], pool_scale[0], w_alpha_up[0], b_alpha[0],
                  g_gla_head[0], w_pool_out[0], w_gla_out[0], w_out[0], g_final, row_tile)


def kernel(x, c, g_norm, w_ada, b_ada, w_in, w_pool_group, pool_scale, w_alpha_up, b_alpha, g_gla_head,
           w_pool_out, w_gla_out, w_out, g_final):
    return _forward(x, c, g_norm, w_ada, b_ada, w_in, w_pool_group, pool_scale, w_alpha_up, b_alpha,
                    g_gla_head, w_pool_out, w_gla_out, w_out, g_final)
```

```python
import functools

import jax
import jax.numpy as jnp
from jax import lax
from jax.experimental import pallas as pl
from jax.experimental.pallas import tpu as pltpu

F32 = jnp.float32
BF16 = jnp.bfloat16

D_MODEL = 1024
EPS = 1e-6
POOL_WINDOWS = (2, 4, 8, 16)
POOL_GROUP_DIM = D_MODEL // len(POOL_WINDOWS)
POOL_HALO = 16
GLA_HEADS = 4
GLA_KEY_DIM = D_MODEL // 2
GLA_VAL_DIM = D_MODEL
GLA_HEAD_K = GLA_KEY_DIM // GLA_HEADS
GLA_HEAD_V = GLA_VAL_DIM // GLA_HEADS
GLA_GATE_RANK = 16
GLA_GATE_NORMALIZER = 16.0
GLA_BLOCK = 128
LANES = 128
SUBLANES = 8
assert max(POOL_WINDOWS) <= 2 * SUBLANES <= POOL_HALO
ROW_TILE = 512
MOD_COL_TILE = 512
VMEM_LIMIT_BYTES = 60 * 1024 * 1024

_IN_SEGMENTS = (("pool_value", D_MODEL), ("pool_gate", D_MODEL), ("q", GLA_KEY_DIM), ("k", GLA_KEY_DIM),
                ("v", GLA_VAL_DIM), ("gla_gate", GLA_VAL_DIM), ("merge_pool", D_MODEL), ("merge_gla", D_MODEL),
                ("decay_gate", LANES))
_IN_COLS = {}
_start = 0
for _name, _width in _IN_SEGMENTS:
    _IN_COLS[_name] = (_start, _start + _width)
    _start += _width
IN_WIDTH_PACKED = _start
_REF_GATE_COL = 2 * D_MODEL + 2 * GLA_KEY_DIM + 2 * GLA_VAL_DIM
IN_WIDTH_REF = _REF_GATE_COL + GLA_GATE_RANK + 2 * D_MODEL
PACK_TILE = 1024
_PACK_SRC_COLS = (0, 1024, 2048, 3072, 4096, _REF_GATE_COL + GLA_GATE_RANK, _REF_GATE_COL + GLA_GATE_RANK + D_MODEL,
                  _REF_GATE_COL)
assert (len(_PACK_SRC_COLS) - 1) * PACK_TILE + LANES == IN_WIDTH_PACKED

_NT = (((1,), (1,)), ((), ()))


def _dot(a, b):
    return jnp.dot(a, b, preferred_element_type=F32)


def _sigmoid(x):
    return 0.5 * jnp.tanh(0.5 * x) + 0.5


def _silu(x):
    half = 0.5 * x
    return half * jnp.tanh(half) + half


def _log_sigmoid(x):
    return jnp.minimum(x, 0.0) - jnp.log1p(jnp.exp(-jnp.abs(x)))


def _mod_kernel(c_ref, w_ref, b_ref, o_ref):
    o_ref[...] = _dot(_silu(c_ref[...]).astype(BF16), w_ref[...].astype(BF16)) + b_ref[...]


def _pack_kernel(src_ref, wt_ref, o_ref):
    i = pl.program_id(0)
    w = wt_ref[...].T
    lane = lax.broadcasted_iota(jnp.int32, w.shape, 1)
    is_gate_tile = i == len(_PACK_SRC_COLS) - 1
    o_ref[...] = jnp.where(jnp.logical_and(is_gate_tile, lane >= GLA_GATE_RANK), 0.0, w).astype(BF16)


def _layer_kernel(x_ref, mod_ref, gnorm_ref, win_ref, wgrp_ref, pscale_ref, wup_ref, balpha_ref, ghead_ref, wpo_ref,
                  wgo_ref, wo_ref, gfin_ref, out_ref, st_ref, halo_ref, o_scr, hb_ref, wkg_ref):
    j = pl.program_id(1)
    ts = x_ref.shape[0]
    d = D_MODEL

    @pl.when(j == 0)
    def _():
        st_ref[...] = jnp.zeros_like(st_ref)
        halo_ref[...] = jnp.zeros_like(halo_ref)

    @pl.when(jnp.logical_and(pl.program_id(0) == 0, j == 0))
    def _():
        k0, k1 = _IN_COLS["k"]
        g0, g1 = _IN_COLS["decay_gate"]
        wkg_ref[:, 0:GLA_KEY_DIM] = win_ref[:, k0:k1]
        wkg_ref[:, GLA_KEY_DIM:GLA_KEY_DIM + LANES] = win_ref[:, g0:g1]

    x = x_ref[...]
    mod = mod_ref[...]
    shift, scale, gate = mod[:, 0:d], mod[:, d:2 * d], mod[:, 2 * d:3 * d]
    h = x * lax.rsqrt(jnp.mean(x * x, axis=-1, keepdims=True) + EPS) * gnorm_ref[...]
    h = h * (1.0 + scale) + shift
    hb_ref[...] = h.astype(BF16)

    def proj(name, rows=slice(None)):
        c0, c1 = _IN_COLS[name]
        return _dot(hb_ref[rows, :], win_ref[:, c0:c1])

    half = ts // 2
    halves = [slice(0, half), slice(half, ts)]
    ka = jnp.concatenate([_dot(hb_ref[hs, :], wkg_ref[...]) for hs in halves], axis=0)
    k = ka[:, 0:GLA_KEY_DIM]
    a_low = ka[:, GLA_KEY_DIM:GLA_KEY_DIM + LANES]
    log_a = _log_sigmoid(_dot(a_low.astype(BF16), wup_ref[...]) + balpha_ref[...]) * (1.0 / GLA_GATE_NORMALIZER)
    la_hi = log_a.astype(BF16)
    la_lo = (log_a - la_hi.astype(F32)).astype(BF16)
    pv = proj("pool_value")
    q = proj("q") * (GLA_HEAD_K ** -0.5)
    v = proj("v")
    rr = lax.broadcasted_iota(jnp.int32, (GLA_BLOCK, GLA_BLOCK), 0)
    cc = lax.broadcasted_iota(jnp.int32, (GLA_BLOCK, GLA_BLOCK), 1)
    causal = rr >= cc
    tri = jnp.where(causal, 1.0, 0.0).astype(BF16)
    nblk = ts // GLA_BLOCK
    cbs = []
    for r in range(nblk):
        rows = slice(r * GLA_BLOCK, (r + 1) * GLA_BLOCK)
        cbs.append(_dot(tri, la_hi[rows, :]) + _dot(tri, la_lo[rows, :]))
    pg = proj("pool_gate")

    t_plus_1 = lax.broadcasted_iota(jnp.int32, (ts, 1), 0) + (j * ts + 1)
    mixed = []
    for g, w in enumerate(POOL_WINDOWS):
        cols = slice(g * POOL_GROUP_DIM, (g + 1) * POOL_GROUP_DIM)
        win = jnp.concatenate([halo_ref[:, cols], pv[:, cols]], axis=0)
        step = 1
        while step < min(w, SUBLANES):
            win = win + pltpu.roll(win, step, axis=0)
            step *= 2
        if w > SUBLANES:
            win = win[POOL_HALO:, :] + win[POOL_HALO - SUBLANES:-SUBLANES, :]
        else:
            win = win[POOL_HALO:, :]
        inv_cnt = 1.0 / jnp.minimum(t_plus_1, w).astype(F32)
        pooled = win * inv_cnt - pv[:, cols]
        mixed.append(_dot(pooled.astype(BF16), wgrp_ref[g]))
    mixed = jnp.concatenate(mixed, axis=1)
    halo_ref[...] = pv[ts - POOL_HALO:, :]
    y_pool = (mixed * pscale_ref[...] * _silu(pg)).astype(BF16)

    fillers = [lambda: _dot(y_pool[halves[0], :], wpo_ref[...]), lambda: _dot(y_pool[halves[1], :], wpo_ref[...]),
               lambda: (proj("merge_pool"), proj("gla_gate")), lambda: proj("merge_gla")]
    filled = []
    for r in range(nblk):
        rows = slice(r * GLA_BLOCK, (r + 1) * GLA_BLOCK)
        cb = cbs[r]
        b_mid = cb[GLA_BLOCK // 2 - 1:GLA_BLOCK // 2, :]
        b_end = cb[GLA_BLOCK - 1:GLA_BLOCK, :]
        q_mid = q[rows, :] * jnp.exp(cb - b_mid)
        k_mid = k[rows, :] * jnp.exp(b_mid - cb)
        q_in = q_mid.astype(BF16)
        k_in = k_mid.astype(BF16)
        q_start = (q_mid * jnp.exp(b_mid)).astype(BF16)
        k_end = (k_mid * jnp.exp(b_end - b_mid)).astype(BF16)
        block_decay = jnp.exp(b_end)
        heads = range(GLA_HEADS)
        kcs = [slice(hh * GLA_HEAD_K, (hh + 1) * GLA_HEAD_K) for hh in heads]
        vcs = [slice(hh * GLA_HEAD_V, (hh + 1) * GLA_HEAD_V) for hh in heads]
        scores = [lax.dot_general(q_in[:, kcs[hh]], k_in[:, kcs[hh]], _NT, preferred_element_type=F32) for hh in heads]
        vbs = [v[rows, vcs[hh]] for hh in heads]
        updates = [_dot(vbs[hh].T.astype(BF16), k_end[:, kcs[hh]]) for hh in heads]
        for hh in heads:
            state_t = st_ref[hh]
            o = _dot(jnp.where(causal, scores[hh], 0.0).astype(BF16), vbs[hh].astype(BF16))
            o = o + lax.dot_general(q_start[:, kcs[hh]], state_t.astype(BF16), _NT, preferred_element_type=F32)
            o_scr[rows, vcs[hh]] = o
            st_ref[hh] = state_t * block_decay[:, kcs[hh]] + updates[hh]
        if r < len(fillers):
            filled.append(fillers[r]())
    for f in fillers[len(filled):]:
        filled.append(f())
    p_a, p_b, (mgp, gg), mgg = filled
    merged = _sigmoid(mgp) * jnp.concatenate([p_a, p_b], axis=0)

    o = o_scr[...]
    normed = []
    for hh in range(GLA_HEADS):
        oh = o[:, hh * GLA_HEAD_V:(hh + 1) * GLA_HEAD_V]
        normed.append(oh * lax.rsqrt(jnp.mean(oh * oh, axis=-1, keepdims=True) + EPS))
    y_gla = (jnp.concatenate(normed, axis=1) * ghead_ref[...] * _silu(gg)).astype(BF16)
    gate_gla = _sigmoid(mgg)
    gfin = gfin_ref[...]
    g_out = [_dot(y_gla[hs, :], wgo_ref[...]) for hs in halves]
    for hs, g_half in zip(halves, g_out):
        merged_half = merged[hs, :] + gate_gla[hs, :] * g_half
        y = x_ref[hs, :] + gate * _dot(merged_half.astype(BF16), wo_ref[...])
        out_ref[hs, :] = y * lax.rsqrt(jnp.mean(y * y, axis=-1, keepdims=True) + EPS) * gfin


def _resident(shape):
    return pl.BlockSpec(shape, lambda b, j: (0,) * len(shape), pipeline_mode=pl.Buffered(1))


def _modulation(c, w_ada, b_ada):
    bsz, d = c.shape
    n = w_ada.shape[1]
    return pl.pallas_call(
        _mod_kernel,
        out_shape=jax.ShapeDtypeStruct((bsz, n), F32),
        grid=(n // MOD_COL_TILE,),
        in_specs=[pl.BlockSpec((bsz, d), lambda i: (0, 0)),
                  pl.BlockSpec((d, MOD_COL_TILE), lambda i: (0, i)),
                  pl.BlockSpec((1, MOD_COL_TILE), lambda i: (0, i))],
        out_specs=pl.BlockSpec((bsz, MOD_COL_TILE), lambda i: (0, i)),
        name="adaln_modulation",
    )(c, w_ada, b_ada.reshape(1, n))


def _layer(x, mod, g_norm, w_in, w_pool_group, pool_scale, w_alpha_up, b_alpha, g_gla_head, w_pool_out,
           w_gla_out, w_out, g_final, row_tile):
    bsz, seq, d = x.shape
    assert d == D_MODEL and seq % row_tile == 0 and row_tile % (2 * GLA_BLOCK) == 0
    ts = row_tile

    assert w_in.shape == (d, IN_WIDTH_REF)
    assert all(c % SUBLANES == 0 for c in _PACK_SRC_COLS)
    src_cols = jnp.asarray([c // SUBLANES for c in _PACK_SRC_COLS], jnp.int32)
    w_packed = pl.pallas_call(
        _pack_kernel,
        out_shape=jax.ShapeDtypeStruct((d, IN_WIDTH_PACKED), BF16),
        grid_spec=pltpu.PrefetchScalarGridSpec(
            num_scalar_prefetch=1,
            grid=(len(_PACK_SRC_COLS),),
            in_specs=[pl.BlockSpec((pl.Element(PACK_TILE), pl.Element(d)), lambda i, src: (src[i] * SUBLANES, 0))],
            out_specs=pl.BlockSpec((d, PACK_TILE), lambda i, src: (0, i)),
        ),
        name="pack_input_projection",
    )(src_cols, w_in.T)
    wup = jnp.pad(w_alpha_up.astype(BF16), ((0, LANES - GLA_GATE_RANK), (0, 0)))

    row = lambda a: a.reshape(1, -1).astype(F32)
    operands = [
        x, mod.reshape(bsz, 1, 3 * d), row(g_norm), w_packed,
        w_pool_group.astype(BF16), row(pool_scale), wup, row(b_alpha), row(jnp.tile(g_gla_head, GLA_HEADS)),
        w_pool_out.astype(BF16), w_gla_out.astype(BF16), w_out.astype(BF16), row(g_final),
    ]
    in_specs = [pl.BlockSpec((None, ts, d), lambda b, j: (b, j, 0)),
                pl.BlockSpec((None, 1, 3 * d), lambda b, j: (b, 0, 0))]
    in_specs += [_resident(a.shape) for a in operands[2:]]

    return pl.pallas_call(
        _layer_kernel,
        out_shape=jax.ShapeDtypeStruct((bsz, seq, d), x.dtype),
        grid=(bsz, seq // ts),
        in_specs=in_specs,
        out_specs=pl.BlockSpec((None, ts, d), lambda b, j: (b, j, 0)),
        scratch_shapes=[pltpu.VMEM((GLA_HEADS, GLA_HEAD_V, GLA_HEAD_K), F32),
                        pltpu.VMEM((POOL_HALO, d), F32),
                        pltpu.VMEM((ts, GLA_VAL_DIM), F32),
                        pltpu.VMEM((ts, d), BF16),
                        pltpu.VMEM((d, GLA_KEY_DIM + LANES), BF16)],
        compiler_params=pltpu.CompilerParams(dimension_semantics=("arbitrary", "arbitrary"),
                                             vmem_limit_bytes=VMEM_LIMIT_BYTES),
        name="hybrid_pool_gla_layer",
    )(*operands)


@functools.partial(jax.jit, static_argnames=("row_tile",))
def _forward(x, c, g_norm, w_ada, b_ada, w_in, w_pool_group, pool_scale, w_alpha_up, b_alpha, g_gla_head,
             w_pool_out, w_gla_out, w_out, g_final, row_tile=ROW_TILE):
    assert g_norm.shape[0] == 1, "single-layer stack"
    mod = _modulation(c, w_ada[0], b_ada[0])
    return _layer(x, mod, g_norm[0], w_in[0], w_pool_group[0], pool_scale[0], w_alpha_up[0], b_alpha[0],
                  g_gla_head[0], w_pool_out[0], w_gla_out[0], w_out[0], g_final, row_tile)


def kernel(x, c, g_norm, w_ada, b_ada, w_in, w_pool_group, pool_scale, w_alpha_up, b_alpha, g_gla_head,
           w_pool_out, w_gla_out, w_out, g_final):
    return _forward(x, c, g_norm, w_ada, b_ada, w_in, w_pool_group, pool_scale, w_alpha_up, b_alpha,
                    g_gla_head, w_pool_out, w_gla_out, w_out, g_final)
```

```python
import functools

import jax
import jax.numpy as jnp
from jax import lax
from jax.experimental import pallas as pl
from jax.experimental.pallas import tpu as pltpu

F32 = jnp.float32
BF16 = jnp.bfloat16

D_MODEL = 1024
EPS = 1e-6
POOL_WINDOWS = (2, 4, 8, 16)
POOL_GROUP_DIM = D_MODEL // len(POOL_WINDOWS)
POOL_HALO = 16
GLA_HEADS = 4
GLA_KEY_DIM = D_MODEL // 2
GLA_VAL_DIM = D_MODEL
GLA_HEAD_K = GLA_KEY_DIM // GLA_HEADS
GLA_HEAD_V = GLA_VAL_DIM // GLA_HEADS
GLA_GATE_RANK = 16
GLA_GATE_NORMALIZER = 16.0
GLA_BLOCK = 128
LANES = 128
SUBLANES = 8
assert max(POOL_WINDOWS) <= 2 * SUBLANES <= POOL_HALO
ROW_TILE = 512
STAGE = 1024
VMEM_LIMIT_BYTES = 60 * 1024 * 1024

_IN_SEGMENTS = (("pool_value", D_MODEL), ("pool_gate", D_MODEL), ("q", GLA_KEY_DIM), ("k", GLA_KEY_DIM),
                ("decay_gate", LANES), ("v", GLA_VAL_DIM), ("gla_gate", GLA_VAL_DIM), ("merge_pool", D_MODEL),
                ("merge_gla", D_MODEL))
_IN_COLS = {}
_start = 0
for _name, _width in _IN_SEGMENTS:
    _IN_COLS[_name] = (_start, _start + _width)
    _start += _width
IN_WIDTH_PACKED = _start
_REF_GATE_COL = 2 * D_MODEL + 2 * GLA_KEY_DIM + 2 * GLA_VAL_DIM
IN_WIDTH_REF = _REF_GATE_COL + GLA_GATE_RANK + 2 * D_MODEL
_IN_CHUNKS = ((0, _IN_COLS["pool_value"][0]), (D_MODEL, _IN_COLS["pool_gate"][0]), (2 * D_MODEL, _IN_COLS["q"][0]),
              (2 * D_MODEL + 2 * GLA_KEY_DIM, _IN_COLS["v"][0]),
              (2 * D_MODEL + 2 * GLA_KEY_DIM + GLA_VAL_DIM, _IN_COLS["gla_gate"][0]),
              (_REF_GATE_COL + GLA_GATE_RANK, _IN_COLS["merge_pool"][0]),
              (_REF_GATE_COL + GLA_GATE_RANK + D_MODEL, _IN_COLS["merge_gla"][0]))
assert _IN_COLS["k"][0] == _IN_COLS["q"][0] + GLA_KEY_DIM and 2 * GLA_KEY_DIM == STAGE == D_MODEL
N_SQUARE = 3
N_MOD = 3

_NT = (((1,), (1,)), ((), ()))


def _dot(a, b):
    return jnp.dot(a, b, preferred_element_type=F32)


def _sigmoid(x):
    return 0.5 * jnp.tanh(0.5 * x) + 0.5


def _silu(x):
    half = 0.5 * x
    return half * jnp.tanh(half) + half


def _log_sigmoid(x):
    return jnp.minimum(x, 0.0) - jnp.log1p(jnp.exp(-jnp.abs(x)))


def _prepare_weights(c_ref, bada_ref, wint_hbm, wada_hbm, wsq_hbms, win_ref, wsq_ref, mod_ref, stage_ref, sem_ref):
    gate_chunk = len(_IN_CHUNKS)
    n_chunks = gate_chunk + 1 + N_SQUARE + N_MOD

    def copy(i):
        slot = i % 2
        if i < gate_chunk:
            src = wint_hbm.at[pl.ds(_IN_CHUNKS[i][0], STAGE), :]
            dst = stage_ref.at[slot]
        elif i == gate_chunk:
            src = wint_hbm.at[pl.ds(_REF_GATE_COL, LANES), :]
            dst = stage_ref.at[slot, pl.ds(0, LANES), :]
        elif i < gate_chunk + 1 + N_SQUARE:
            src = wsq_hbms[i - gate_chunk - 1]
            dst = stage_ref.at[slot]
        else:
            src = wada_hbm.at[:, pl.ds((i - gate_chunk - 1 - N_SQUARE) * STAGE, STAGE)]
            dst = stage_ref.at[slot]
        return pltpu.make_async_copy(src, dst, sem_ref.at[slot])

    silu_c = _silu(c_ref[...]).astype(BF16)
    copy(0).start()
    for i in range(n_chunks):
        if i + 1 < n_chunks:
            copy(i + 1).start()
        copy(i).wait()
        slot = i % 2
        if i < gate_chunk:
            dst0 = _IN_CHUNKS[i][1]
            win_ref[:, dst0:dst0 + STAGE] = stage_ref[slot].T.astype(BF16)
        elif i == gate_chunk:
            tile = stage_ref[slot, 0:LANES, :].T
            lane = lax.broadcasted_iota(jnp.int32, tile.shape, 1)
            g0, g1 = _IN_COLS["decay_gate"]
            win_ref[:, g0:g1] = jnp.where(lane < GLA_GATE_RANK, tile, 0.0).astype(BF16)
        elif i < gate_chunk + 1 + N_SQUARE:
            wsq_ref[i - gate_chunk - 1] = stage_ref[slot].astype(BF16)
        else:
            c0 = (i - gate_chunk - 1 - N_SQUARE) * STAGE
            mod_ref[:, c0:c0 + STAGE] = _dot(silu_c, stage_ref[slot].astype(BF16)) + bada_ref[:, c0:c0 + STAGE]


def _layer_kernel(x_ref, c_ref, bada_ref, gnorm_ref, wgrp_ref, pscale_ref, wup_ref, balpha_ref, ghead_ref, gfin_ref,
                  wint_hbm, wada_hbm, wpo_hbm, wgo_hbm, wo_hbm, out_ref,
                  win_ref, wsq_ref, mod_ref, stage_ref, sem_ref, st_ref, halo_ref, o_scr, hb_ref):
    b = pl.program_id(0)
    j = pl.program_id(1)
    ts = x_ref.shape[0]
    d = D_MODEL

    @pl.when(jnp.logical_and(b == 0, j == 0))
    def _():
        _prepare_weights(c_ref, bada_ref, wint_hbm, wada_hbm, (wpo_hbm, wgo_hbm, wo_hbm), win_ref, wsq_ref, mod_ref,
                         stage_ref, sem_ref)

    @pl.when(j == 0)
    def _():
        st_ref[...] = jnp.zeros_like(st_ref)
        halo_ref[...] = jnp.zeros_like(halo_ref)

    wpo_ref, wgo_ref, wo_ref = wsq_ref.at[0], wsq_ref.at[1], wsq_ref.at[2]
    x = x_ref[...]
    mod = mod_ref[pl.ds(b, 1), :]
    shift, scale, gate = mod[:, 0:d], mod[:, d:2 * d], mod[:, 2 * d:3 * d]
    h = x * lax.rsqrt(jnp.mean(x * x, axis=-1, keepdims=True) + EPS) * gnorm_ref[...]
    h = h * (1.0 + scale) + shift
    hb_ref[...] = h.astype(BF16)

    def proj(c0, c1, rows=slice(None)):
        return _dot(hb_ref[rows, :], win_ref[:, c0:c1])

    def proj_seg(name):
        return proj(*_IN_COLS[name])

    half = ts // 2
    halves = [slice(0, half), slice(half, ts)]
    k0, g1 = _IN_COLS["k"][0], _IN_COLS["decay_gate"][1]
    ka = jnp.concatenate([proj(k0, g1, hs) for hs in halves], axis=0)
    k = ka[:, 0:GLA_KEY_DIM]
    a_low = ka[:, GLA_KEY_DIM:GLA_KEY_DIM + LANES]
    log_a = _log_sigmoid(_dot(a_low.astype(BF16), wup_ref[...]) + balpha_ref[...]) * (1.0 / GLA_GATE_NORMALIZER)
    la_hi = log_a.astype(BF16)
    la_lo = (log_a - la_hi.astype(F32)).astype(BF16)
    q = proj_seg("q") * (GLA_HEAD_K ** -0.5)
    v = proj_seg("v")
    rr = lax.broadcasted_iota(jnp.int32, (GLA_BLOCK, GLA_BLOCK), 0)
    cc = lax.broadcasted_iota(jnp.int32, (GLA_BLOCK, GLA_BLOCK), 1)
    causal = rr >= cc
    tri = jnp.where(causal, 1.0, 0.0).astype(BF16)
    tri2 = jnp.concatenate([tri, tri], axis=1)
    nblk = ts // GLA_BLOCK
    cbs = []
    for r in range(nblk):
        rows = slice(r * GLA_BLOCK, (r + 1) * GLA_BLOCK)
        cbs.append(_dot(tri2, jnp.concatenate([la_hi[rows, :], la_lo[rows, :]], axis=0)))
    pv = proj_seg("pool_value")

    fillers = ["pool_gate", "merge_pool", "gla_gate", "merge_gla"]
    filled = []
    for r in range(nblk):
        rows = slice(r * GLA_BLOCK, (r + 1) * GLA_BLOCK)
        cb = cbs[r]
        b_mid = cb[GLA_BLOCK // 2 - 1:GLA_BLOCK // 2, :]
        b_end = cb[GLA_BLOCK - 1:GLA_BLOCK, :]
        q_mid = q[rows, :] * jnp.exp(cb - b_mid)
        k_mid = k[rows, :] * jnp.exp(b_mid - cb)
        q_in = q_mid.astype(BF16)
        k_in = k_mid.astype(BF16)
        q_start = (q_mid * jnp.exp(b_mid)).astype(BF16)
        k_end = (k_mid * jnp.exp(b_end - b_mid)).astype(BF16)
        block_decay = jnp.exp(b_end)
        heads = range(GLA_HEADS)
        kcs = [slice(hh * GLA_HEAD_K, (hh + 1) * GLA_HEAD_K) for hh in heads]
        vcs = [slice(hh * GLA_HEAD_V, (hh + 1) * GLA_HEAD_V) for hh in heads]
        scores = [lax.dot_general(q_in[:, kcs[hh]], k_in[:, kcs[hh]], _NT, preferred_element_type=F32) for hh in heads]
        vbs = [v[rows, vcs[hh]] for hh in heads]
        updates = [_dot(vbs[hh].T.astype(BF16), k_end[:, kcs[hh]]) for hh in heads]
        for hh in heads:
            state_t = st_ref[hh]
            o = _dot(jnp.where(causal, scores[hh], 0.0).astype(BF16), vbs[hh].astype(BF16))
            o = o + lax.dot_general(q_start[:, kcs[hh]], state_t.astype(BF16), _NT, preferred_element_type=F32)
            o_scr[rows, vcs[hh]] = o
            st_ref[hh] = state_t * block_decay[:, kcs[hh]] + updates[hh]
        if r < len(fillers):
            filled.append(proj_seg(fillers[r]))
    for name in fillers[len(filled):]:
        filled.append(proj_seg(name))
    pg, mgp, gg, mgg = filled

    t_plus_1 = lax.broadcasted_iota(jnp.int32, (ts, 1), 0) + (j * ts + 1)
    mixed = []
    for g, w in enumerate(POOL_WINDOWS):
        cols = slice(g * POOL_GROUP_DIM, (g + 1) * POOL_GROUP_DIM)
        win = jnp.concatenate([halo_ref[:, cols], pv[:, cols]], axis=0)
        step = 1
        while step < min(w, SUBLANES):
            win = win + pltpu.roll(win, step, axis=0)
            step *= 2
        if w > SUBLANES:
            win = win[POOL_HALO:, :] + win[POOL_HALO - SUBLANES:-SUBLANES, :]
        else:
            win = win[POOL_HALO:, :]
        inv_cnt = 1.0 / jnp.minimum(t_plus_1, w).astype(F32)
        pooled = win * inv_cnt - pv[:, cols]
        mixed.append(_dot(pooled.astype(BF16), wgrp_ref[g]))
    mixed = jnp.concatenate(mixed, axis=1)
    halo_ref[...] = pv[ts - POOL_HALO:, :]
    y_pool = mixed * pscale_ref[...] * _silu(pg)
    merged = _sigmoid(mgp) * _dot(y_pool.astype(BF16), wpo_ref[...])

    o = o_scr[...]
    normed = []
    for hh in range(GLA_HEADS):
        oh = o[:, hh * GLA_HEAD_V:(hh + 1) * GLA_HEAD_V]
        normed.append(oh * lax.rsqrt(jnp.mean(oh * oh, axis=-1, keepdims=True) + EPS))
    y_gla = (jnp.concatenate(normed, axis=1) * ghead_ref[...] * _silu(gg)).astype(BF16)
    gate_gla = _sigmoid(mgg)
    gfin = gfin_ref[...]
    g_out = [_dot(y_gla[hs, :], wgo_ref[...]) for hs in halves]
    for hs, g_half in zip(halves, g_out):
        merged_half = merged[hs, :] + gate_gla[hs, :] * g_half
        y = x_ref[hs, :] + gate * _dot(merged_half.astype(BF16), wo_ref[...])
        out_ref[hs, :] = y * lax.rsqrt(jnp.mean(y * y, axis=-1, keepdims=True) + EPS) * gfin


def _resident(shape):
    return pl.BlockSpec(shape, lambda b, j: (0,) * len(shape), pipeline_mode=pl.Buffered(1))


@functools.partial(jax.jit, static_argnames=("row_tile",))
def _forward(x, c, g_norm, w_ada, b_ada, w_in, w_pool_group, pool_scale, w_alpha_up, b_alpha, g_gla_head,
             w_pool_out, w_gla_out, w_out, g_final, row_tile=ROW_TILE):
    assert g_norm.shape[0] == 1, "single-layer stack"
    bsz, seq, d = x.shape
    ts = row_tile
    assert d == D_MODEL and seq % ts == 0 and ts % (2 * GLA_BLOCK) == 0
    assert w_in.shape == (1, d, IN_WIDTH_REF) and w_ada.shape == (1, d, N_MOD * STAGE)

    row = lambda a: a.reshape(1, -1).astype(F32)
    wup = jnp.pad(w_alpha_up[0].astype(BF16), ((0, LANES - GLA_GATE_RANK), (0, 0)))
    vmem_operands = [
        c, row(b_ada), row(g_norm), w_pool_group[0].astype(BF16), row(pool_scale), wup, row(b_alpha),
        row(jnp.tile(g_gla_head[0], GLA_HEADS)), row(g_final),
    ]
    hbm_operands = [w_in[0].T, w_ada[0], w_pool_out[0], w_gla_out[0], w_out[0]]
    in_specs = [pl.BlockSpec((None, ts, d), lambda b, j: (b, j, 0))]
    in_specs += [_resident(a.shape) for a in vmem_operands]
    in_specs += [pl.BlockSpec(memory_space=pl.ANY) for _ in hbm_operands]

    return pl.pallas_call(
        _layer_kernel,
        out_shape=jax.ShapeDtypeStruct((bsz, seq, d), x.dtype),
        grid=(bsz, seq // ts),
        in_specs=in_specs,
        out_specs=pl.BlockSpec((None, ts, d), lambda b, j: (b, j, 0)),
        scratch_shapes=[pltpu.VMEM((d, IN_WIDTH_PACKED), BF16),
                        pltpu.VMEM((N_SQUARE, d, d), BF16),
                        pltpu.VMEM((bsz, N_MOD * d), F32),
                        pltpu.VMEM((2, STAGE, STAGE), F32),
                        pltpu.SemaphoreType.DMA((2,)),
                        pltpu.VMEM((GLA_HEADS, GLA_HEAD_V, GLA_HEAD_K), F32),
                        pltpu.VMEM((POOL_HALO, d), F32),
                        pltpu.VMEM((ts, GLA_VAL_DIM), F32),
                        pltpu.VMEM((ts, d), BF16)],
        compiler_params=pltpu.CompilerParams(dimension_semantics=("arbitrary", "arbitrary"),
                                             vmem_limit_bytes=VMEM_LIMIT_BYTES),
        name="hybrid_pool_gla_layer",
    )(x, *vmem_operands, *hbm_operands)


def kernel(x, c, g_norm, w_ada, b_ada, w_in, w_pool_group, pool_scale, w_alpha_up, b_alpha, g_gla_head,
           w_pool_out, w_gla_out, w_out, g_final):
    return _forward(x, c, g_norm, w_ada, b_ada, w_in, w_pool_group, pool_scale, w_alpha_up, b_alpha,
                    g_gla_head, w_pool_out, w_gla_out, w_out, g_final)
```

```python
import functools

import jax
import jax.numpy as jnp
from jax import lax
from jax.experimental import pallas as pl
from jax.experimental.pallas import tpu as pltpu

F32 = jnp.float32
BF16 = jnp.bfloat16

D_MODEL = 1024
EPS = 1e-6
POOL_WINDOWS = (2, 4, 8, 16)
POOL_GROUP_DIM = D_MODEL // len(POOL_WINDOWS)
POOL_HALO = 16
GLA_HEADS = 4
GLA_KEY_DIM = D_MODEL // 2
GLA_VAL_DIM = D_MODEL
GLA_HEAD_K = GLA_KEY_DIM // GLA_HEADS
GLA_HEAD_V = GLA_VAL_DIM // GLA_HEADS
GLA_GATE_RANK = 16
GLA_GATE_NORMALIZER = 16.0
GLA_BLOCK = 128
LANES = 128
SUBLANES = 8
assert max(POOL_WINDOWS) <= 2 * SUBLANES <= POOL_HALO
ROW_TILE = 512
STAGE = 1024
VMEM_LIMIT_BYTES = 60 * 1024 * 1024

_IN_SEGMENTS = (("pool_value", D_MODEL), ("pool_gate", D_MODEL), ("q", GLA_KEY_DIM), ("k", GLA_KEY_DIM),
                ("decay_gate", LANES), ("v", GLA_VAL_DIM), ("gla_gate", GLA_VAL_DIM), ("merge_pool", D_MODEL),
                ("merge_gla", D_MODEL))
_IN_COLS = {}
_start = 0
for _name, _width in _IN_SEGMENTS:
    _IN_COLS[_name] = (_start, _start + _width)
    _start += _width
IN_WIDTH_PACKED = _start
_REF_GATE_COL = 2 * D_MODEL + 2 * GLA_KEY_DIM + 2 * GLA_VAL_DIM
IN_WIDTH_REF = _REF_GATE_COL + GLA_GATE_RANK + 2 * D_MODEL
_IN_CHUNKS = ((0, _IN_COLS["pool_value"][0]), (D_MODEL, _IN_COLS["pool_gate"][0]), (2 * D_MODEL, _IN_COLS["q"][0]),
              (2 * D_MODEL + 2 * GLA_KEY_DIM, _IN_COLS["v"][0]),
              (2 * D_MODEL + 2 * GLA_KEY_DIM + GLA_VAL_DIM, _IN_COLS["gla_gate"][0]),
              (_REF_GATE_COL + GLA_GATE_RANK, _IN_COLS["merge_pool"][0]),
              (_REF_GATE_COL + GLA_GATE_RANK + D_MODEL, _IN_COLS["merge_gla"][0]))
assert _IN_COLS["k"][0] == _IN_COLS["q"][0] + GLA_KEY_DIM and 2 * GLA_KEY_DIM == STAGE == D_MODEL
N_SQUARE = 3
N_MOD = 3

_NT = (((1,), (1,)), ((), ()))


def _dot(a, b):
    return jnp.dot(a, b, preferred_element_type=F32)


def _sigmoid(x):
    return 0.5 * jnp.tanh(0.5 * x) + 0.5


def _silu(x):
    half = 0.5 * x
    return half * jnp.tanh(half) + half


def _log_sigmoid(x):
    return jnp.minimum(x, 0.0) - jnp.log1p(jnp.exp(-jnp.abs(x)))


def _prepare_weights(c_ref, bada_ref, wint_hbm, wada_hbm, wsq_hbms, win_ref, wsq_ref, mod_ref, stage_ref, sem_ref):
    gate_chunk = len(_IN_CHUNKS)
    n_chunks = gate_chunk + 1 + N_SQUARE + N_MOD

    def copy(i):
        slot = i % 2
        if i < gate_chunk:
            src = wint_hbm.at[pl.ds(_IN_CHUNKS[i][0], STAGE), :]
            dst = stage_ref.at[slot]
        elif i == gate_chunk:
            src = wint_hbm.at[pl.ds(_REF_GATE_COL, LANES), :]
            dst = stage_ref.at[slot, pl.ds(0, LANES), :]
        elif i < gate_chunk + 1 + N_SQUARE:
            src = wsq_hbms[i - gate_chunk - 1]
            dst = stage_ref.at[slot]
        else:
            src = wada_hbm.at[:, pl.ds((i - gate_chunk - 1 - N_SQUARE) * STAGE, STAGE)]
            dst = stage_ref.at[slot]
        return pltpu.make_async_copy(src, dst, sem_ref.at[slot])

    silu_c = _silu(c_ref[...]).astype(BF16)
    copy(0).start()
    for i in range(n_chunks):
        if i + 1 < n_chunks:
            copy(i + 1).start()
        copy(i).wait()
        slot = i % 2
        if i < gate_chunk:
            dst0 = _IN_CHUNKS[i][1]
            win_ref[:, dst0:dst0 + STAGE] = stage_ref[slot].T.astype(BF16)
        elif i == gate_chunk:
            tile = stage_ref[slot, 0:LANES, :].T
            lane = lax.broadcasted_iota(jnp.int32, tile.shape, 1)
            g0, g1 = _IN_COLS["decay_gate"]
            win_ref[:, g0:g1] = jnp.where(lane < GLA_GATE_RANK, tile, 0.0).astype(BF16)
        elif i < gate_chunk + 1 + N_SQUARE:
            wsq_ref[i - gate_chunk - 1] = stage_ref[slot].astype(BF16)
        else:
            c0 = (i - gate_chunk - 1 - N_SQUARE) * STAGE
            mod_ref[:, c0:c0 + STAGE] = _dot(silu_c, stage_ref[slot].astype(BF16)) + bada_ref[:, c0:c0 + STAGE]


def _layer_kernel(x_ref, c_ref, bada_ref, gnorm_ref, wgrp_ref, pscale_ref, wup_ref, balpha_ref, ghead_ref, gfin_ref,
                  wint_hbm, wada_hbm, wpo_hbm, wgo_hbm, wo_hbm, out_ref,
                  win_ref, wsq_ref, mod_ref, stage_ref, sem_ref, st_ref, halo_ref, o_scr, hb_ref):
    b = pl.program_id(0)
    j = pl.program_id(1)
    ts = x_ref.shape[0]
    d = D_MODEL

    @pl.when(jnp.logical_and(b == 0, j == 0))
    def _():
        _prepare_weights(c_ref, bada_ref, wint_hbm, wada_hbm, (wpo_hbm, wgo_hbm, wo_hbm), win_ref, wsq_ref, mod_ref,
                         stage_ref, sem_ref)

    @pl.when(j == 0)
    def _():
        st_ref[...] = jnp.zeros_like(st_ref)
        halo_ref[...] = jnp.zeros_like(halo_ref)

    wpo_ref, wgo_ref, wo_ref = wsq_ref.at[0], wsq_ref.at[1], wsq_ref.at[2]
    x = x_ref[...]
    mod = mod_ref[pl.ds(b, 1), :]
    shift, scale, gate = mod[:, 0:d], mod[:, d:2 * d], mod[:, 2 * d:3 * d]
    h = x * lax.rsqrt(jnp.mean(x * x, axis=-1, keepdims=True) + EPS) * gnorm_ref[...]
    h = h * (1.0 + scale) + shift
    hb_ref[...] = h.astype(BF16)

    def proj(c0, c1, rows=slice(None)):
        return _dot(hb_ref[rows, :], win_ref[:, c0:c1])

    def proj_seg(name):
        return proj(*_IN_COLS[name])

    half = ts // 2
    halves = [slice(0, half), slice(half, ts)]
    k0, g1 = _IN_COLS["k"][0], _IN_COLS["decay_gate"][1]
    ka = jnp.concatenate([proj(k0, g1, hs) for hs in halves], axis=0)
    k = ka[:, 0:GLA_KEY_DIM]
    a_low = ka[:, GLA_KEY_DIM:GLA_KEY_DIM + LANES]
    log_a = _log_sigmoid(_dot(a_low.astype(BF16), wup_ref[...]) + balpha_ref[...]) * (1.0 / GLA_GATE_NORMALIZER)
    la_hi = log_a.astype(BF16)
    la_lo = (log_a - la_hi.astype(F32)).astype(BF16)
    q = proj_seg("q") * (GLA_HEAD_K ** -0.5)
    v = proj_seg("v")
    rr = lax.broadcasted_iota(jnp.int32, (GLA_BLOCK, GLA_BLOCK), 0)
    cc = lax.broadcasted_iota(jnp.int32, (GLA_BLOCK, GLA_BLOCK), 1)
    causal = rr >= cc
    tri = jnp.where(causal, 1.0, 0.0).astype(BF16)
    tri2 = jnp.concatenate([tri, tri], axis=1)
    nblk = ts // GLA_BLOCK
    cbs = []
    for r in range(nblk):
        rows = slice(r * GLA_BLOCK, (r + 1) * GLA_BLOCK)
        cbs.append(_dot(tri2, jnp.concatenate([la_hi[rows, :], la_lo[rows, :]], axis=0)))
    pv = proj_seg("pool_value")

    assert nblk <= SUBLANES
    sub = lax.broadcasted_iota(jnp.int32, (SUBLANES, GLA_KEY_DIM), 0)
    dec_rows = jnp.zeros((SUBLANES, GLA_KEY_DIM), F32)
    for r in range(nblk):
        dec_rows = jnp.where(sub == r, jnp.exp(cbs[r][GLA_BLOCK - 1:GLA_BLOCK, :]), dec_rows)
    dec_rows = jnp.concatenate([dec_rows, jnp.zeros((GLA_HEAD_K - SUBLANES, GLA_KEY_DIM), F32)], axis=0)
    dec_cols = [dec_rows[:, hh * GLA_HEAD_K:(hh + 1) * GLA_HEAD_K].T for hh in range(GLA_HEADS)]
    fillers = ["pool_gate", "merge_pool", "gla_gate", "merge_gla"]
    filled = []
    for r in range(nblk):
        rows = slice(r * GLA_BLOCK, (r + 1) * GLA_BLOCK)
        cb = cbs[r]
        b_mid = cb[GLA_BLOCK // 2 - 1:GLA_BLOCK // 2, :]
        b_end = cb[GLA_BLOCK - 1:GLA_BLOCK, :]
        q_mid = q[rows, :] * jnp.exp(cb - b_mid)
        k_mid = k[rows, :] * jnp.exp(b_mid - cb)
        q_in = q_mid.astype(BF16)
        k_in = k_mid.astype(BF16)
        q_start = (q_mid * jnp.exp(b_mid)).astype(BF16)
        k_end = k_mid * jnp.exp(b_end - b_mid)
        heads = range(GLA_HEADS)
        kcs = [slice(hh * GLA_HEAD_K, (hh + 1) * GLA_HEAD_K) for hh in heads]
        vcs = [slice(hh * GLA_HEAD_V, (hh + 1) * GLA_HEAD_V) for hh in heads]
        scores = [lax.dot_general(q_in[:, kcs[hh]], k_in[:, kcs[hh]], _NT, preferred_element_type=F32) for hh in heads]
        vbs = [v[rows, vcs[hh]].astype(BF16) for hh in heads]
        updates = [_dot(k_end[:, kcs[hh]].T.astype(BF16), vbs[hh]) for hh in heads]
        for hh in heads:
            state = st_ref[hh]
            lhs = jnp.concatenate([jnp.where(causal, scores[hh], 0.0).astype(BF16), q_start[:, kcs[hh]]], axis=1)
            rhs = jnp.concatenate([vbs[hh], state.astype(BF16)], axis=0)
            o_scr[rows, vcs[hh]] = _dot(lhs, rhs)
            st_ref[hh] = state * dec_cols[hh][:, r:r + 1] + updates[hh]
        if r < len(fillers):
            filled.append(proj_seg(fillers[r]))
    for name in fillers[len(filled):]:
        filled.append(proj_seg(name))
    pg, mgp, gg, mgg = filled

    t_plus_1 = lax.broadcasted_iota(jnp.int32, (ts, 1), 0) + (j * ts + 1)
    mixed = []
    for g, w in enumerate(POOL_WINDOWS):
        cols = slice(g * POOL_GROUP_DIM, (g + 1) * POOL_GROUP_DIM)
        win = jnp.concatenate([halo_ref[:, cols], pv[:, cols]], axis=0)
        step = 1
        while step < min(w, SUBLANES):
            win = win + pltpu.roll(win, step, axis=0)
            step *= 2
        if w > SUBLANES:
            win = win[POOL_HALO:, :] + win[POOL_HALO - SUBLANES:-SUBLANES, :]
        else:
            win = win[POOL_HALO:, :]
        inv_cnt = 1.0 / jnp.minimum(t_plus_1, w).astype(F32)
        pooled = win * inv_cnt - pv[:, cols]
        mixed.append(_dot(pooled.astype(BF16), wgrp_ref[g]))
    mixed = jnp.concatenate(mixed, axis=1)
    halo_ref[...] = pv[ts - POOL_HALO:, :]
    y_pool = mixed * pscale_ref[...] * _silu(pg)
    merged = _sigmoid(mgp) * _dot(y_pool.astype(BF16), wpo_ref[...])

    o = o_scr[...]
    normed = []
    for hh in range(GLA_HEADS):
        oh = o[:, hh * GLA_HEAD_V:(hh + 1) * GLA_HEAD_V]
        normed.append(oh * lax.rsqrt(jnp.mean(oh * oh, axis=-1, keepdims=True) + EPS))
    y_gla = (jnp.concatenate(normed, axis=1) * ghead_ref[...] * _silu(gg)).astype(BF16)
    gate_gla = _sigmoid(mgg)
    gfin = gfin_ref[...]
    g_out = [_dot(y_gla[hs, :], wgo_ref[...]) for hs in halves]
    for hs, g_half in zip(halves, g_out):
        merged_half = merged[hs, :] + gate_gla[hs, :] * g_half
        y = x_ref[hs, :] + gate * _dot(merged_half.astype(BF16), wo_ref[...])
        out_ref[hs, :] = y * lax.rsqrt(jnp.mean(y * y, axis=-1, keepdims=True) + EPS) * gfin


def _resident(shape):
    return pl.BlockSpec(shape, lambda b, j: (0,) * len(shape), pipeline_mode=pl.Buffered(1))


@functools.partial(jax.jit, static_argnames=("row_tile",))
def _forward(x, c, g_norm, w_ada, b_ada, w_in, w_pool_group, pool_scale, w_alpha_up, b_alpha, g_gla_head,
             w_pool_out, w_gla_out, w_out, g_final, row_tile=ROW_TILE):
    assert g_norm.shape[0] == 1, "single-layer stack"
    bsz, seq, d = x.shape
    ts = row_tile
    assert d == D_MODEL and seq % ts == 0 and ts % (2 * GLA_BLOCK) == 0
    assert w_in.shape == (1, d, IN_WIDTH_REF) and w_ada.shape == (1, d, N_MOD * STAGE)

    row = lambda a: a.reshape(1, -1).astype(F32)
    wup = jnp.pad(w_alpha_up[0].astype(BF16), ((0, LANES - GLA_GATE_RANK), (0, 0)))
    vmem_operands = [
        c, row(b_ada), row(g_norm), w_pool_group[0].astype(BF16), row(pool_scale), wup, row(b_alpha),
        row(jnp.tile(g_gla_head[0], GLA_HEADS)), row(g_final),
    ]
    hbm_operands = [w_in[0].T, w_ada[0], w_pool_out[0], w_gla_out[0], w_out[0]]
    in_specs = [pl.BlockSpec((None, ts, d), lambda b, j: (b, j, 0))]
    in_specs += [_resident(a.shape) for a in vmem_operands]
    in_specs += [pl.BlockSpec(memory_space=pl.ANY) for _ in hbm_operands]

    return pl.pallas_call(
        _layer_kernel,
        out_shape=jax.ShapeDtypeStruct((bsz, seq, d), x.dtype),
        grid=(bsz, seq // ts),
        in_specs=in_specs,
        out_specs=pl.BlockSpec((None, ts, d), lambda b, j: (b, j, 0)),
        scratch_shapes=[pltpu.VMEM((d, IN_WIDTH_PACKED), BF16),
                        pltpu.VMEM((N_SQUARE, d, d), BF16),
                        pltpu.VMEM((bsz, N_MOD * d), F32),
                        pltpu.VMEM((2, STAGE, STAGE), F32),
                        pltpu.SemaphoreType.DMA((2,)),
                        pltpu.VMEM((GLA_HEADS, GLA_HEAD_K, GLA_HEAD_V), F32),
                        pltpu.VMEM((POOL_HALO, d), F32),
                        pltpu.VMEM((ts, GLA_VAL_DIM), F32),
                        pltpu.VMEM((ts, d), BF16)],
        compiler_params=pltpu.CompilerParams(dimension_semantics=("arbitrary", "arbitrary"),
                                             vmem_limit_bytes=VMEM_LIMIT_BYTES),
        name="hybrid_pool_gla_layer",
    )(x, *vmem_operands, *hbm_operands)


def kernel(x, c, g_norm, w_ada, b_ada, w_in, w_pool_group, pool_scale, w_alpha_up, b_alpha, g_gla_head,
           w_pool_out, w_gla_out, w_out, g_final):
    return _forward(x, c, g_norm, w_ada, b_ada, w_in, w_pool_group, pool_scale, w_alpha_up, b_alpha,
                    g_gla_head, w_pool_out, w_gla_out, w_out, g_final)
```

```python
import functools

import jax
import jax.numpy as jnp
from jax import lax
from jax.experimental import pallas as pl
from jax.experimental.pallas import tpu as pltpu

F32 = jnp.float32
BF16 = jnp.bfloat16

D_MODEL = 1024
EPS = 1e-6
POOL_WINDOWS = (2, 4, 8, 16)
POOL_GROUP_DIM = D_MODEL // len(POOL_WINDOWS)
POOL_HALO = 16
GLA_HEADS = 4
GLA_KEY_DIM = D_MODEL // 2
GLA_VAL_DIM = D_MODEL
GLA_HEAD_K = GLA_KEY_DIM // GLA_HEADS
GLA_HEAD_V = GLA_VAL_DIM // GLA_HEADS
GLA_GATE_RANK = 16
GLA_GATE_NORMALIZER = 16.0
GLA_BLOCK = 128
LANES = 128
SUBLANES = 8
assert max(POOL_WINDOWS) <= 2 * SUBLANES <= POOL_HALO
ROW_TILE = 512
STAGE = 1024
VMEM_LIMIT_BYTES = 60 * 1024 * 1024

_IN_SEGMENTS = (("pool_value", D_MODEL), ("pool_gate", D_MODEL), ("q", GLA_KEY_DIM), ("k", GLA_KEY_DIM),
                ("decay_gate", LANES), ("v", GLA_VAL_DIM), ("gla_gate", GLA_VAL_DIM), ("merge_pool", D_MODEL),
                ("merge_gla", D_MODEL))
_IN_COLS = {}
_start = 0
for _name, _width in _IN_SEGMENTS:
    _IN_COLS[_name] = (_start, _start + _width)
    _start += _width
IN_WIDTH_PACKED = _start
_REF_GATE_COL = 2 * D_MODEL + 2 * GLA_KEY_DIM + 2 * GLA_VAL_DIM
IN_WIDTH_REF = _REF_GATE_COL + GLA_GATE_RANK + 2 * D_MODEL
_IN_CHUNKS = ((0, _IN_COLS["pool_value"][0]), (D_MODEL, _IN_COLS["pool_gate"][0]), (2 * D_MODEL, _IN_COLS["q"][0]),
              (2 * D_MODEL + 2 * GLA_KEY_DIM, _IN_COLS["v"][0]),
              (2 * D_MODEL + 2 * GLA_KEY_DIM + GLA_VAL_DIM, _IN_COLS["gla_gate"][0]),
              (_REF_GATE_COL + GLA_GATE_RANK, _IN_COLS["merge_pool"][0]),
              (_REF_GATE_COL + GLA_GATE_RANK + D_MODEL, _IN_COLS["merge_gla"][0]))
assert _IN_COLS["k"][0] == _IN_COLS["q"][0] + GLA_KEY_DIM and 2 * GLA_KEY_DIM == STAGE == D_MODEL
N_SQUARE = 3
N_MOD = 3

_NT = (((1,), (1,)), ((), ()))


def _dot(a, b):
    return jnp.dot(a, b, preferred_element_type=F32)


def _sigmoid(x):
    return 0.5 * jnp.tanh(0.5 * x) + 0.5


def _silu(x):
    half = 0.5 * x
    return half * jnp.tanh(half) + half


def _log_sigmoid(x):
    return jnp.minimum(x, 0.0) - jnp.log1p(jnp.exp(-jnp.abs(x)))


def _prepare_weights(c_ref, bada_ref, wint_hbm, wada_hbm, wsq_hbms, win_ref, wsq_ref, mod_ref, stage_ref, sem_ref):
    gate_chunk = len(_IN_CHUNKS)
    n_chunks = gate_chunk + 1 + N_SQUARE + N_MOD

    def copy(i):
        slot = i % 2
        if i < gate_chunk:
            src = wint_hbm.at[pl.ds(_IN_CHUNKS[i][0], STAGE), :]
            dst = stage_ref.at[slot]
        elif i == gate_chunk:
            src = wint_hbm.at[pl.ds(_REF_GATE_COL, LANES), :]
            dst = stage_ref.at[slot, pl.ds(0, LANES), :]
        elif i < gate_chunk + 1 + N_SQUARE:
            src = wsq_hbms[i - gate_chunk - 1]
            dst = stage_ref.at[slot]
        else:
            src = wada_hbm.at[:, pl.ds((i - gate_chunk - 1 - N_SQUARE) * STAGE, STAGE)]
            dst = stage_ref.at[slot]
        return pltpu.make_async_copy(src, dst, sem_ref.at[slot])

    silu_c = _silu(c_ref[...]).astype(BF16)
    copy(0).start()
    for i in range(n_chunks):
        if i + 1 < n_chunks:
            copy(i + 1).start()
        copy(i).wait()
        slot = i % 2
        if i < gate_chunk:
            dst0 = _IN_CHUNKS[i][1]
            win_ref[:, dst0:dst0 + STAGE] = stage_ref[slot].T.astype(BF16)
        elif i == gate_chunk:
            tile = stage_ref[slot, 0:LANES, :].T
            lane = lax.broadcasted_iota(jnp.int32, tile.shape, 1)
            g0, g1 = _IN_COLS["decay_gate"]
            win_ref[:, g0:g1] = jnp.where(lane < GLA_GATE_RANK, tile, 0.0).astype(BF16)
        elif i < gate_chunk + 1 + N_SQUARE:
            wsq_ref[i - gate_chunk - 1] = stage_ref[slot].astype(BF16)
        else:
            c0 = (i - gate_chunk - 1 - N_SQUARE) * STAGE
            mod_ref[:, c0:c0 + STAGE] = _dot(silu_c, stage_ref[slot].astype(BF16)) + bada_ref[:, c0:c0 + STAGE]


def _layer_kernel(x_ref, c_ref, bada_ref, gnorm_ref, wgrp_ref, pscale_ref, wup_ref, balpha_ref, ghead_ref, gfin_ref,
                  wint_hbm, wada_hbm, wpo_hbm, wgo_hbm, wo_hbm, out_ref,
                  win_ref, wsq_ref, mod_ref, stage_ref, sem_ref, st_ref, halo_ref, o_scr, hb_ref):
    b = pl.program_id(0)
    j = pl.program_id(1)
    ts = x_ref.shape[0]
    d = D_MODEL

    @pl.when(jnp.logical_and(b == 0, j == 0))
    def _():
        _prepare_weights(c_ref, bada_ref, wint_hbm, wada_hbm, (wpo_hbm, wgo_hbm, wo_hbm), win_ref, wsq_ref, mod_ref,
                         stage_ref, sem_ref)

    @pl.when(j == 0)
    def _():
        st_ref[...] = jnp.zeros_like(st_ref)
        halo_ref[...] = jnp.zeros_like(halo_ref)

    wpo_ref, wgo_ref, wo_ref = wsq_ref.at[0], wsq_ref.at[1], wsq_ref.at[2]
    x = x_ref[...]
    mod = mod_ref[pl.ds(b, 1), :]
    shift, scale, gate = mod[:, 0:d], mod[:, d:2 * d], mod[:, 2 * d:3 * d]
    h = x * lax.rsqrt(jnp.mean(x * x, axis=-1, keepdims=True) + EPS) * gnorm_ref[...]
    h = h * (1.0 + scale) + shift
    hb_ref[...] = h.astype(BF16)

    def proj(c0, c1, rows=slice(None)):
        return _dot(hb_ref[rows, :], win_ref[:, c0:c1])

    def proj_seg(name):
        return proj(*_IN_COLS[name])

    half = ts // 2
    halves = [slice(0, half), slice(half, ts)]
    k0, g1 = _IN_COLS["k"][0], _IN_COLS["decay_gate"][1]
    ka = jnp.concatenate([proj(k0, g1, hs) for hs in halves], axis=0)
    k = ka[:, 0:GLA_KEY_DIM]
    a_low = ka[:, GLA_KEY_DIM:GLA_KEY_DIM + LANES]
    wup = jnp.concatenate([wup_ref[...].astype(BF16), jnp.zeros((LANES - GLA_GATE_RANK, GLA_KEY_DIM), BF16)], axis=0)
    log_a = _log_sigmoid(_dot(a_low.astype(BF16), wup) + balpha_ref[...]) * (1.0 / GLA_GATE_NORMALIZER)
    la_hi = log_a.astype(BF16)
    la_lo = (log_a - la_hi.astype(F32)).astype(BF16)
    q = proj_seg("q") * (GLA_HEAD_K ** -0.5)
    v = proj_seg("v")
    rr = lax.broadcasted_iota(jnp.int32, (GLA_BLOCK, GLA_BLOCK), 0)
    cc = lax.broadcasted_iota(jnp.int32, (GLA_BLOCK, GLA_BLOCK), 1)
    causal = rr >= cc
    tri = jnp.where(causal, 1.0, 0.0).astype(BF16)
    tri2 = jnp.concatenate([tri, tri], axis=1)
    nblk = ts // GLA_BLOCK
    cbs = []
    for r in range(nblk):
        rows = slice(r * GLA_BLOCK, (r + 1) * GLA_BLOCK)
        cbs.append(_dot(tri2, jnp.concatenate([la_hi[rows, :], la_lo[rows, :]], axis=0)))
    pv = proj_seg("pool_value")

    fillers = ["pool_gate", "merge_pool", "gla_gate", "merge_gla"]
    filled = []
    for r in range(nblk):
        rows = slice(r * GLA_BLOCK, (r + 1) * GLA_BLOCK)
        cb = cbs[r]
        b_mid = cb[GLA_BLOCK // 2 - 1:GLA_BLOCK // 2, :]
        b_end = cb[GLA_BLOCK - 1:GLA_BLOCK, :]
        q_mid = q[rows, :] * jnp.exp(cb - b_mid)
        k_mid = k[rows, :] * jnp.exp(b_mid - cb)
        q_in = q_mid.astype(BF16)
        k_in = k_mid.astype(BF16)
        q_start = (q_mid * jnp.exp(b_mid)).astype(BF16)
        k_end = (k_mid * jnp.exp(b_end - b_mid)).astype(BF16)
        block_decay = jnp.exp(b_end)
        heads = range(GLA_HEADS)
        kcs = [slice(hh * GLA_HEAD_K, (hh + 1) * GLA_HEAD_K) for hh in heads]
        vcs = [slice(hh * GLA_HEAD_V, (hh + 1) * GLA_HEAD_V) for hh in heads]
        scores = [lax.dot_general(q_in[:, kcs[hh]], k_in[:, kcs[hh]], _NT, preferred_element_type=F32) for hh in heads]
        vbs = [v[rows, vcs[hh]] for hh in heads]
        updates = [_dot(vbs[hh].T.astype(BF16), k_end[:, kcs[hh]]) for hh in heads]
        for hh in heads:
            state_t = st_ref[hh]
            o = _dot(jnp.where(causal, scores[hh], 0.0).astype(BF16), vbs[hh].astype(BF16))
            o = o + lax.dot_general(q_start[:, kcs[hh]], state_t.astype(BF16), _NT, preferred_element_type=F32)
            o_scr[rows, vcs[hh]] = o
            st_ref[hh] = state_t * block_decay[:, kcs[hh]] + updates[hh]
        if r < len(fillers):
            filled.append(proj_seg(fillers[r]))
    for name in fillers[len(filled):]:
        filled.append(proj_seg(name))
    pg, mgp, gg, mgg = filled

    t_plus_1 = lax.broadcasted_iota(jnp.int32, (ts, 1), 0) + (j * ts + 1)
    mixed = []
    for g, w in enumerate(POOL_WINDOWS):
        cols = slice(g * POOL_GROUP_DIM, (g + 1) * POOL_GROUP_DIM)
        win = jnp.concatenate([halo_ref[:, cols], pv[:, cols]], axis=0)
        step = 1
        while step < min(w, SUBLANES):
            win = win + pltpu.roll(win, step, axis=0)
            step *= 2
        if w > SUBLANES:
            win = win[POOL_HALO:, :] + win[POOL_HALO - SUBLANES:-SUBLANES, :]
        else:
            win = win[POOL_HALO:, :]
        inv_cnt = 1.0 / jnp.minimum(t_plus_1, w).astype(F32)
        pooled = win * inv_cnt - pv[:, cols]
        mixed.append(_dot(pooled.astype(BF16), wgrp_ref[g].astype(BF16)))
    mixed = jnp.concatenate(mixed, axis=1)
    halo_ref[...] = pv[ts - POOL_HALO:, :]
    y_pool = mixed * pscale_ref[...] * _silu(pg)
    merged = _sigmoid(mgp) * _dot(y_pool.astype(BF16), wpo_ref[...])

    o = o_scr[...]
    normed = []
    for hh in range(GLA_HEADS):
        oh = o[:, hh * GLA_HEAD_V:(hh + 1) * GLA_HEAD_V]
        normed.append(oh * lax.rsqrt(jnp.mean(oh * oh, axis=-1, keepdims=True) + EPS) * ghead_ref[...])
    y_gla = (jnp.concatenate(normed, axis=1) * _silu(gg)).astype(BF16)
    gate_gla = _sigmoid(mgg)
    gfin = gfin_ref[...]
    g_out = [_dot(y_gla[hs, :], wgo_ref[...]) for hs in halves]
    for hs, g_half in zip(halves, g_out):
        merged_half = merged[hs, :] + gate_gla[hs, :] * g_half
        y = x_ref[hs, :] + gate * _dot(merged_half.astype(BF16), wo_ref[...])
        out_ref[hs, :] = y * lax.rsqrt(jnp.mean(y * y, axis=-1, keepdims=True) + EPS) * gfin


def _resident(shape):
    return pl.BlockSpec(shape, lambda b, j: (0,) * len(shape), pipeline_mode=pl.Buffered(1))


@functools.partial(jax.jit, static_argnames=("row_tile",))
def _forward(x, c, g_norm, w_ada, b_ada, w_in, w_pool_group, pool_scale, w_alpha_up, b_alpha, g_gla_head,
             w_pool_out, w_gla_out, w_out, g_final, row_tile=ROW_TILE):
    assert g_norm.shape[0] == 1, "single-layer stack"
    bsz, seq, d = x.shape
    ts = row_tile
    assert d == D_MODEL and seq % ts == 0 and ts % (2 * GLA_BLOCK) == 0
    assert w_in.shape == (1, d, IN_WIDTH_REF) and w_ada.shape == (1, d, N_MOD * STAGE)

    row = lambda a: a.reshape(1, -1).astype(F32)
    vmem_operands = [
        c, row(b_ada), row(g_norm), w_pool_group[0], row(pool_scale), w_alpha_up[0], row(b_alpha),
        row(g_gla_head), row(g_final),
    ]
    hbm_operands = [w_in[0].T, w_ada[0], w_pool_out[0], w_gla_out[0], w_out[0]]
    in_specs = [pl.BlockSpec((None, ts, d), lambda b, j: (b, j, 0))]
    in_specs += [_resident(a.shape) for a in vmem_operands]
    in_specs += [pl.BlockSpec(memory_space=pl.ANY) for _ in hbm_operands]

    return pl.pallas_call(
        _layer_kernel,
        out_shape=jax.ShapeDtypeStruct((bsz, seq, d), x.dtype),
        grid=(bsz, seq // ts),
        in_specs=in_specs,
        out_specs=pl.BlockSpec((None, ts, d), lambda b, j: (b, j, 0)),
        scratch_shapes=[pltpu.VMEM((d, IN_WIDTH_PACKED), BF16),
                        pltpu.VMEM((N_SQUARE, d, d), BF16),
                        pltpu.VMEM((bsz, N_MOD * d), F32),
                        pltpu.VMEM((2, STAGE, STAGE), F32),
                        pltpu.SemaphoreType.DMA((2,)),
                        pltpu.VMEM((GLA_HEADS, GLA_HEAD_V, GLA_HEAD_K), F32),
                        pltpu.VMEM((POOL_HALO, d), F32),
                        pltpu.VMEM((ts, GLA_VAL_DIM), F32),
                        pltpu.VMEM((ts, d), BF16)],
        compiler_params=pltpu.CompilerParams(dimension_semantics=("arbitrary", "arbitrary"),
                                             vmem_limit_bytes=VMEM_LIMIT_BYTES),
        name="hybrid_pool_gla_layer",
    )(x, *vmem_operands, *hbm_operands)


def kernel(x, c, g_norm, w_ada, b_ada, w_in, w_pool_group, pool_scale, w_alpha_up, b_alpha, g_gla_head,
           w_pool_out, w_gla_out, w_out, g_final):
    return _forward(x, c, g_norm, w_ada, b_ada, w_in, w_pool_group, pool_scale, w_alpha_up, b_alpha,
                    g_gla_head, w_pool_out, w_gla_out, w_out, g_final)
```

```python
import functools

import jax
import jax.numpy as jnp
from jax import lax
from jax.experimental import pallas as pl
from jax.experimental.pallas import tpu as pltpu

F32 = jnp.float32
BF16 = jnp.bfloat16

D_MODEL = 1024
EPS = 1e-6
POOL_WINDOWS = (2, 4, 8, 16)
POOL_GROUP_DIM = D_MODEL // len(POOL_WINDOWS)
POOL_HALO = 16
GLA_HEADS = 4
GLA_KEY_DIM = D_MODEL // 2
GLA_VAL_DIM = D_MODEL
GLA_HEAD_K = GLA_KEY_DIM // GLA_HEADS
GLA_HEAD_V = GLA_VAL_DIM // GLA_HEADS
GLA_GATE_RANK = 16
GLA_GATE_NORMALIZER = 16.0
GLA_BLOCK = 128
LANES = 128
SUBLANES = 8
assert max(POOL_WINDOWS) <= 2 * SUBLANES <= POOL_HALO
ROW_TILE = 512
STAGE = 1024
VMEM_LIMIT_BYTES = 60 * 1024 * 1024

_IN_SEGMENTS = (("pool_value", D_MODEL), ("pool_gate", D_MODEL), ("q", GLA_KEY_DIM), ("k", GLA_KEY_DIM),
                ("decay_gate", LANES), ("v", GLA_VAL_DIM), ("gla_gate", GLA_VAL_DIM), ("merge_pool", D_MODEL),
                ("merge_gla", D_MODEL))
_IN_COLS = {}
_start = 0
for _name, _width in _IN_SEGMENTS:
    _IN_COLS[_name] = (_start, _start + _width)
    _start += _width
IN_WIDTH_PACKED = _start
_REF_GATE_COL = 2 * D_MODEL + 2 * GLA_KEY_DIM + 2 * GLA_VAL_DIM
IN_WIDTH_REF = _REF_GATE_COL + GLA_GATE_RANK + 2 * D_MODEL
_IN_CHUNKS = ((0, _IN_COLS["pool_value"][0]), (D_MODEL, _IN_COLS["pool_gate"][0]), (2 * D_MODEL, _IN_COLS["q"][0]),
              (2 * D_MODEL + 2 * GLA_KEY_DIM, _IN_COLS["v"][0]),
              (2 * D_MODEL + 2 * GLA_KEY_DIM + GLA_VAL_DIM, _IN_COLS["gla_gate"][0]),
              (_REF_GATE_COL + GLA_GATE_RANK, _IN_COLS["merge_pool"][0]),
              (_REF_GATE_COL + GLA_GATE_RANK + D_MODEL, _IN_COLS["merge_gla"][0]))
assert _IN_COLS["k"][0] == _IN_COLS["q"][0] + GLA_KEY_DIM and 2 * GLA_KEY_DIM == STAGE == D_MODEL
N_SQUARE = 3
N_MOD = 3

_NT = (((1,), (1,)), ((), ()))


def _dot(a, b):
    return jnp.dot(a, b, preferred_element_type=F32)


def _sigmoid(x):
    return 0.5 * jnp.tanh(0.5 * x) + 0.5


def _silu(x):
    half = 0.5 * x
    return half * jnp.tanh(half) + half


def _log_sigmoid(x):
    return jnp.minimum(x, 0.0) - jnp.log1p(jnp.exp(-jnp.abs(x)))


def _prepare_weights(c_ref, bada_ref, wint_hbm, wada_hbm, wsq_hbms, win_ref, wsq_ref, mod_ref, stage_ref, sem_ref):
    gate_chunk = len(_IN_CHUNKS)
    n_chunks = gate_chunk + 1 + N_SQUARE + N_MOD

    def copy(i):
        slot = i % 2
        if i < gate_chunk:
            src = wint_hbm.at[pl.ds(_IN_CHUNKS[i][0], STAGE), :]
            dst = stage_ref.at[slot]
        elif i == gate_chunk:
            src = wint_hbm.at[pl.ds(_REF_GATE_COL, LANES), :]
            dst = stage_ref.at[slot, pl.ds(0, LANES), :]
        elif i < gate_chunk + 1 + N_SQUARE:
            src = wsq_hbms[i - gate_chunk - 1]
            dst = stage_ref.at[slot]
        else:
            src = wada_hbm.at[:, pl.ds((i - gate_chunk - 1 - N_SQUARE) * STAGE, STAGE)]
            dst = stage_ref.at[slot]
        return pltpu.make_async_copy(src, dst, sem_ref.at[slot])

    silu_c = _silu(c_ref[...]).astype(BF16)
    copy(0).start()
    for i in range(n_chunks):
        if i + 1 < n_chunks:
            copy(i + 1).start()
        copy(i).wait()
        slot = i % 2
        if i < gate_chunk:
            dst0 = _IN_CHUNKS[i][1]
            win_ref[:, dst0:dst0 + STAGE] = stage_ref[slot].T.astype(BF16)
        elif i == gate_chunk:
            tile = stage_ref[slot, 0:LANES, :].T
            lane = lax.broadcasted_iota(jnp.int32, tile.shape, 1)
            g0, g1 = _IN_COLS["decay_gate"]
            win_ref[:, g0:g1] = jnp.where(lane < GLA_GATE_RANK, tile, 0.0).astype(BF16)
        elif i < gate_chunk + 1 + N_SQUARE:
            wsq_ref[i - gate_chunk - 1] = stage_ref[slot].astype(BF16)
        else:
            c0 = (i - gate_chunk - 1 - N_SQUARE) * STAGE
            mod_ref[:, c0:c0 + STAGE] = _dot(silu_c, stage_ref[slot].astype(BF16)) + bada_ref[:, c0:c0 + STAGE]


def _modulated_input(x, mod_row, gnorm):
    d = D_MODEL
    shift, scale = mod_row[:, 0:d], mod_row[:, d:2 * d]
    h = x * lax.rsqrt(jnp.mean(x * x, axis=-1, keepdims=True) + EPS) * gnorm
    return (h * (1.0 + scale) + shift).astype(BF16)


def _layer_kernel(x_ref, xn_ref, c_ref, bada_ref, gnorm_ref, wgrp_ref, pscale_ref, wup_ref, balpha_ref, ghead_ref, gfin_ref,
                  wint_hbm, wada_hbm, wpo_hbm, wgo_hbm, wo_hbm, out_ref,
                  win_ref, wsq_ref, mod_ref, stage_ref, sem_ref, st_ref, halo_ref, o_scr, hb_ref):
    b = pl.program_id(0)
    j = pl.program_id(1)
    ts = x_ref.shape[0]
    d = D_MODEL

    @pl.when(jnp.logical_and(b == 0, j == 0))
    def _():
        _prepare_weights(c_ref, bada_ref, wint_hbm, wada_hbm, (wpo_hbm, wgo_hbm, wo_hbm), win_ref, wsq_ref, mod_ref,
                         stage_ref, sem_ref)
        hb_ref[...] = _modulated_input(x_ref[...], mod_ref[0:1, :], gnorm_ref[...])

    @pl.when(j == 0)
    def _():
        st_ref[...] = jnp.zeros_like(st_ref)
        halo_ref[...] = jnp.zeros_like(halo_ref)

    wpo_ref, wgo_ref, wo_ref = wsq_ref.at[0], wsq_ref.at[1], wsq_ref.at[2]
    gate = mod_ref[pl.ds(b, 1), 2 * d:3 * d]

    def proj(c0, c1, rows=slice(None)):
        return _dot(hb_ref[rows, :], win_ref[:, c0:c1])

    def proj_seg(name):
        return proj(*_IN_COLS[name])

    half = ts // 2
    halves = [slice(0, half), slice(half, ts)]
    k0, g1 = _IN_COLS["k"][0], _IN_COLS["decay_gate"][1]
    ka = jnp.concatenate([proj(k0, g1, hs) for hs in halves], axis=0)
    k = ka[:, 0:GLA_KEY_DIM]
    a_low = ka[:, GLA_KEY_DIM:GLA_KEY_DIM + LANES]
    wup = jnp.concatenate([wup_ref[...].astype(BF16), jnp.zeros((LANES - GLA_GATE_RANK, GLA_KEY_DIM), BF16)], axis=0)
    log_a = _log_sigmoid(_dot(a_low.astype(BF16), wup) + balpha_ref[...]) * (1.0 / GLA_GATE_NORMALIZER)
    la_hi = log_a.astype(BF16)
    la_lo = (log_a - la_hi.astype(F32)).astype(BF16)
    q = proj_seg("q") * (GLA_HEAD_K ** -0.5)
    v = proj_seg("v")
    rr = lax.broadcasted_iota(jnp.int32, (GLA_BLOCK, GLA_BLOCK), 0)
    cc = lax.broadcasted_iota(jnp.int32, (GLA_BLOCK, GLA_BLOCK), 1)
    causal = rr >= cc
    tri = jnp.where(causal, 1.0, 0.0).astype(BF16)
    tri2 = jnp.concatenate([tri, tri], axis=1)
    nblk = ts // GLA_BLOCK
    cbs = []
    for r in range(nblk):
        rows = slice(r * GLA_BLOCK, (r + 1) * GLA_BLOCK)
        cbs.append(_dot(tri2, jnp.concatenate([la_hi[rows, :], la_lo[rows, :]], axis=0)))
    pv = proj_seg("pool_value")

    fillers = ["pool_gate", "merge_pool", "gla_gate", "merge_gla"]
    filled = []
    for r in range(nblk):
        rows = slice(r * GLA_BLOCK, (r + 1) * GLA_BLOCK)
        cb = cbs[r]
        b_mid = cb[GLA_BLOCK // 2 - 1:GLA_BLOCK // 2, :]
        b_end = cb[GLA_BLOCK - 1:GLA_BLOCK, :]
        q_mid = q[rows, :] * jnp.exp(cb - b_mid)
        k_mid = k[rows, :] * jnp.exp(b_mid - cb)
        q_in = q_mid.astype(BF16)
        k_in = k_mid.astype(BF16)
        q_start = (q_mid * jnp.exp(b_mid)).astype(BF16)
        k_end = (k_mid * jnp.exp(b_end - b_mid)).astype(BF16)
        block_decay = jnp.exp(b_end)
        heads = range(GLA_HEADS)
        kcs = [slice(hh * GLA_HEAD_K, (hh + 1) * GLA_HEAD_K) for hh in heads]
        vcs = [slice(hh * GLA_HEAD_V, (hh + 1) * GLA_HEAD_V) for hh in heads]
        scores = [lax.dot_general(q_in[:, kcs[hh]], k_in[:, kcs[hh]], _NT, preferred_element_type=F32) for hh in heads]
        vbs = [v[rows, vcs[hh]] for hh in heads]
        updates = [_dot(vbs[hh].T.astype(BF16), k_end[:, kcs[hh]]) for hh in heads]
        for hh in heads:
            state_t = st_ref[hh]
            o = _dot(jnp.where(causal, scores[hh], 0.0).astype(BF16), vbs[hh].astype(BF16))
            o = o + lax.dot_general(q_start[:, kcs[hh]], state_t.astype(BF16), _NT, preferred_element_type=F32)
            o_scr[rows, vcs[hh]] = o
            st_ref[hh] = state_t * block_decay[:, kcs[hh]] + updates[hh]
        if r < len(fillers):
            filled.append(proj_seg(fillers[r]))
    for name in fillers[len(filled):]:
        filled.append(proj_seg(name))
    pg, mgp, gg, mgg = filled

    t_plus_1 = lax.broadcasted_iota(jnp.int32, (ts, 1), 0) + (j * ts + 1)
    mixed = []
    for g, w in enumerate(POOL_WINDOWS):
        cols = slice(g * POOL_GROUP_DIM, (g + 1) * POOL_GROUP_DIM)
        win = jnp.concatenate([halo_ref[:, cols], pv[:, cols]], axis=0)
        step = 1
        while step < min(w, SUBLANES):
            win = win + pltpu.roll(win, step, axis=0)
            step *= 2
        if w > SUBLANES:
            win = win[POOL_HALO:, :] + win[POOL_HALO - SUBLANES:-SUBLANES, :]
        else:
            win = win[POOL_HALO:, :]
        inv_cnt = 1.0 / jnp.minimum(t_plus_1, w).astype(F32)
        pooled = win * inv_cnt - pv[:, cols]
        mixed.append(_dot(pooled.astype(BF16), wgrp_ref[g].astype(BF16)))
    mixed = jnp.concatenate(mixed, axis=1)
    halo_ref[...] = pv[ts - POOL_HALO:, :]
    y_pool = mixed * pscale_ref[...] * _silu(pg)
    merged = _sigmoid(mgp) * _dot(y_pool.astype(BF16), wpo_ref[...])

    o = o_scr[...]
    normed = []
    for hh in range(GLA_HEADS):
        oh = o[:, hh * GLA_HEAD_V:(hh + 1) * GLA_HEAD_V]
        normed.append(oh * lax.rsqrt(jnp.mean(oh * oh, axis=-1, keepdims=True) + EPS) * ghead_ref[...])
    y_gla = (jnp.concatenate(normed, axis=1) * _silu(gg)).astype(BF16)
    gate_gla = _sigmoid(mgg)
    gfin = gfin_ref[...]
    g_out = [_dot(y_gla[hs, :], wgo_ref[...]) for hs in halves]
    for hs, g_half in zip(halves, g_out):
        merged_half = merged[hs, :] + gate_gla[hs, :] * g_half
        y = x_ref[hs, :] + gate * _dot(merged_half.astype(BF16), wo_ref[...])
        out_ref[hs, :] = y * lax.rsqrt(jnp.mean(y * y, axis=-1, keepdims=True) + EPS) * gfin

    wraps = j + 1 >= pl.num_programs(1)
    b_next = jnp.where(wraps, jnp.minimum(b + 1, pl.num_programs(0) - 1), b)
    hb_ref[...] = _modulated_input(xn_ref[...], mod_ref[pl.ds(b_next, 1), :], gnorm_ref[...])


def _resident(shape):
    return pl.BlockSpec(shape, lambda b, j: (0,) * len(shape), pipeline_mode=pl.Buffered(1))


@functools.partial(jax.jit, static_argnames=("row_tile",))
def _forward(x, c, g_norm, w_ada, b_ada, w_in, w_pool_group, pool_scale, w_alpha_up, b_alpha, g_gla_head,
             w_pool_out, w_gla_out, w_out, g_final, row_tile=ROW_TILE):
    assert g_norm.shape[0] == 1, "single-layer stack"
    bsz, seq, d = x.shape
    ts = row_tile
    assert d == D_MODEL and seq % ts == 0 and ts % (2 * GLA_BLOCK) == 0
    assert w_in.shape == (1, d, IN_WIDTH_REF) and w_ada.shape == (1, d, N_MOD * STAGE)

    row = lambda a: a.reshape(1, -1).astype(F32)
    vmem_operands = [
        c, row(b_ada), row(g_norm), w_pool_group[0], row(pool_scale), w_alpha_up[0], row(b_alpha),
        row(g_gla_head), row(g_final),
    ]
    hbm_operands = [w_in[0].T, w_ada[0], w_pool_out[0], w_gla_out[0], w_out[0]]
    nj = seq // ts

    def next_tile(b, j):
        wraps = j + 1 >= nj
        return jnp.where(wraps, jnp.minimum(b + 1, bsz - 1), b), jnp.where(wraps, 0, j + 1), 0

    in_specs = [pl.BlockSpec((None, ts, d), lambda b, j: (b, j, 0)), pl.BlockSpec((None, ts, d), next_tile)]
    in_specs += [_resident(a.shape) for a in vmem_operands]
    in_specs += [pl.BlockSpec(memory_space=pl.ANY) for _ in hbm_operands]

    return pl.pallas_call(
        _layer_kernel,
        out_shape=jax.ShapeDtypeStruct((bsz, seq, d), x.dtype),
        grid=(bsz, seq // ts),
        in_specs=in_specs,
        out_specs=pl.BlockSpec((None, ts, d), lambda b, j: (b, j, 0)),
        scratch_shapes=[pltpu.VMEM((d, IN_WIDTH_PACKED), BF16),
                        pltpu.VMEM((N_SQUARE, d, d), BF16),
                        pltpu.VMEM((bsz, N_MOD * d), F32),
                        pltpu.VMEM((2, STAGE, STAGE), F32),
                        pltpu.SemaphoreType.DMA((2,)),
                        pltpu.VMEM((GLA_HEADS, GLA_HEAD_V, GLA_HEAD_K), F32),
                        pltpu.VMEM((POOL_HALO, d), F32),
                        pltpu.VMEM((ts, GLA_VAL_DIM), F32),
                        pltpu.VMEM((ts, d), BF16)],
        compiler_params=pltpu.CompilerParams(dimension_semantics=("arbitrary", "arbitrary"),
                                             vmem_limit_bytes=VMEM_LIMIT_BYTES),
        name="hybrid_pool_gla_layer",
    )(x, x, *vmem_operands, *hbm_operands)


def kernel(x, c, g_norm, w_ada, b_ada, w_in, w_pool_group, pool_scale, w_alpha_up, b_alpha, g_gla_head,
           w_pool_out, w_gla_out, w_out, g_final):
    return _forward(x, c, g_norm, w_ada, b_ada, w_in, w_pool_group, pool_scale, w_alpha_up, b_alpha,
                    g_gla_head, w_pool_out, w_gla_out, w_out, g_final)
```

```python
import functools

import jax
import jax.numpy as jnp
from jax import lax
from jax.experimental import pallas as pl
from jax.experimental.pallas import tpu as pltpu

F32 = jnp.float32
BF16 = jnp.bfloat16

D_MODEL = 1024
EPS = 1e-6
POOL_WINDOWS = (2, 4, 8, 16)
POOL_GROUP_DIM = D_MODEL // len(POOL_WINDOWS)
POOL_HALO = 16
GLA_HEADS = 4
GLA_KEY_DIM = D_MODEL // 2
GLA_VAL_DIM = D_MODEL
GLA_HEAD_K = GLA_KEY_DIM // GLA_HEADS
GLA_HEAD_V = GLA_VAL_DIM // GLA_HEADS
GLA_GATE_RANK = 16
GLA_GATE_NORMALIZER = 16.0
GLA_BLOCK = 128
LANES = 128
SUBLANES = 8
assert max(POOL_WINDOWS) <= 2 * SUBLANES <= POOL_HALO
ROW_TILE = 512
STAGE = 1024
VMEM_LIMIT_BYTES = 60 * 1024 * 1024

_IN_SEGMENTS = (("pool_value", D_MODEL), ("pool_gate", D_MODEL), ("q", GLA_KEY_DIM), ("k", GLA_KEY_DIM),
                ("decay_gate", LANES), ("v", GLA_VAL_DIM), ("gla_gate", GLA_VAL_DIM), ("merge_pool", D_MODEL),
                ("merge_gla", D_MODEL))
_IN_COLS = {}
_start = 0
for _name, _width in _IN_SEGMENTS:
    _IN_COLS[_name] = (_start, _start + _width)
    _start += _width
IN_WIDTH_PACKED = _start
_REF_GATE_COL = 2 * D_MODEL + 2 * GLA_KEY_DIM + 2 * GLA_VAL_DIM
IN_WIDTH_REF = _REF_GATE_COL + GLA_GATE_RANK + 2 * D_MODEL
_IN_CHUNKS = ((0, _IN_COLS["pool_value"][0]), (D_MODEL, _IN_COLS["pool_gate"][0]), (2 * D_MODEL, _IN_COLS["q"][0]),
              (2 * D_MODEL + 2 * GLA_KEY_DIM, _IN_COLS["v"][0]),
              (2 * D_MODEL + 2 * GLA_KEY_DIM + GLA_VAL_DIM, _IN_COLS["gla_gate"][0]),
              (_REF_GATE_COL + GLA_GATE_RANK, _IN_COLS["merge_pool"][0]),
              (_REF_GATE_COL + GLA_GATE_RANK + D_MODEL, _IN_COLS["merge_gla"][0]))
assert _IN_COLS["k"][0] == _IN_COLS["q"][0] + GLA_KEY_DIM and 2 * GLA_KEY_DIM == STAGE == D_MODEL
N_SQUARE = 3
N_MOD = 3

_NT = (((1,), (1,)), ((), ()))


def _dot(a, b):
    return jnp.dot(a, b, preferred_element_type=F32)


def _sigmoid(x):
    return 0.5 * jnp.tanh(0.5 * x) + 0.5


def _silu(x):
    half = 0.5 * x
    return half * jnp.tanh(half) + half


def _log_sigmoid(x):
    return jnp.minimum(x, 0.0) - jnp.log1p(jnp.exp(-jnp.abs(x)))


def _prepare_weights(c_ref, bada_ref, wint_hbm, wada_hbm, wsq_hbms, win_ref, wsq_ref, mod_ref, stage_ref, sem_ref):
    gate_chunk = len(_IN_CHUNKS)
    n_chunks = gate_chunk + 1 + N_SQUARE + N_MOD

    def copy(i):
        slot = i % 2
        if i < gate_chunk:
            src = wint_hbm.at[pl.ds(_IN_CHUNKS[i][0], STAGE), :]
            dst = stage_ref.at[slot]
        elif i == gate_chunk:
            src = wint_hbm.at[pl.ds(_REF_GATE_COL, LANES), :]
            dst = stage_ref.at[slot, pl.ds(0, LANES), :]
        elif i < gate_chunk + 1 + N_SQUARE:
            src = wsq_hbms[i - gate_chunk - 1]
            dst = stage_ref.at[slot]
        else:
            src = wada_hbm.at[:, pl.ds((i - gate_chunk - 1 - N_SQUARE) * STAGE, STAGE)]
            dst = stage_ref.at[slot]
        return pltpu.make_async_copy(src, dst, sem_ref.at[slot])

    silu_c = _silu(c_ref[...]).astype(BF16)
    copy(0).start()
    for i in range(n_chunks):
        if i + 1 < n_chunks:
            copy(i + 1).start()
        copy(i).wait()
        slot = i % 2
        if i < gate_chunk:
            dst0 = _IN_CHUNKS[i][1]
            win_ref[:, dst0:dst0 + STAGE] = stage_ref[slot].T.astype(BF16)
        elif i == gate_chunk:
            tile = stage_ref[slot, 0:LANES, :].T
            lane = lax.broadcasted_iota(jnp.int32, tile.shape, 1)
            g0, g1 = _IN_COLS["decay_gate"]
            win_ref[:, g0:g1] = jnp.where(lane < GLA_GATE_RANK, tile, 0.0).astype(BF16)
        elif i < gate_chunk + 1 + N_SQUARE:
            wsq_ref[i - gate_chunk - 1] = stage_ref[slot].astype(BF16)
        else:
            c0 = (i - gate_chunk - 1 - N_SQUARE) * STAGE
            mod_ref[:, c0:c0 + STAGE] = _dot(silu_c, stage_ref[slot].astype(BF16)) + bada_ref[:, c0:c0 + STAGE]


def _layer_kernel(x_ref, c_ref, bada_ref, gnorm_ref, wgrp_ref, pscale_ref, wup_ref, balpha_ref, ghead_ref, gfin_ref,
                  wint_hbm, wada_hbm, wpo_hbm, wgo_hbm, wo_hbm, out_ref,
                  win_ref, wsq_ref, mod_ref, stage_ref, sem_ref, st_ref, halo_ref, o_scr, hb_ref):
    b = pl.program_id(0)
    j = pl.program_id(1)
    ts = x_ref.shape[0]
    d = D_MODEL

    @pl.when(jnp.logical_and(b == 0, j == 0))
    def _():
        _prepare_weights(c_ref, bada_ref, wint_hbm, wada_hbm, (wpo_hbm, wgo_hbm, wo_hbm), win_ref, wsq_ref, mod_ref,
                         stage_ref, sem_ref)

    @pl.when(j == 0)
    def _():
        st_ref[...] = jnp.zeros_like(st_ref)
        halo_ref[...] = jnp.zeros_like(halo_ref)

    wpo_ref, wgo_ref, wo_ref = wsq_ref.at[0], wsq_ref.at[1], wsq_ref.at[2]
    x = x_ref[...]
    mod = mod_ref[pl.ds(b, 1), :]
    shift, scale, gate = mod[:, 0:d], mod[:, d:2 * d], mod[:, 2 * d:3 * d]
    h = x * lax.rsqrt(jnp.mean(x * x, axis=-1, keepdims=True) + EPS) * gnorm_ref[...]
    h = h * (1.0 + scale) + shift
    hb_ref[...] = h.astype(BF16)

    def proj(c0, c1, rows=slice(None)):
        return _dot(hb_ref[rows, :], win_ref[:, c0:c1])

    def proj_seg(name):
        return proj(*_IN_COLS[name])

    half = ts // 2
    halves = [slice(0, half), slice(half, ts)]
    k0, g1 = _IN_COLS["k"][0], _IN_COLS["decay_gate"][1]
    ka = proj(k0, g1)
    k = ka[:, 0:GLA_KEY_DIM]
    a_low = ka[:, GLA_KEY_DIM:GLA_KEY_DIM + LANES]
    wup = jnp.concatenate([wup_ref[...].astype(BF16), jnp.zeros((LANES - GLA_GATE_RANK, GLA_KEY_DIM), BF16)], axis=0)
    log_a = _log_sigmoid(_dot(a_low.astype(BF16), wup) + balpha_ref[...]) * (1.0 / GLA_GATE_NORMALIZER)
    la_hi = log_a.astype(BF16)
    la_lo = (log_a - la_hi.astype(F32)).astype(BF16)
    q = proj_seg("q") * (GLA_HEAD_K ** -0.5)
    v = proj_seg("v")
    rr = lax.broadcasted_iota(jnp.int32, (GLA_BLOCK, GLA_BLOCK), 0)
    cc = lax.broadcasted_iota(jnp.int32, (GLA_BLOCK, GLA_BLOCK), 1)
    causal = rr >= cc
    tri = jnp.where(causal, 1.0, 0.0).astype(BF16)
    tri2 = jnp.concatenate([tri, tri], axis=1)
    nblk = ts // GLA_BLOCK
    cbs = []
    for r in range(nblk):
        rows = slice(r * GLA_BLOCK, (r + 1) * GLA_BLOCK)
        cbs.append(_dot(tri2, jnp.concatenate([la_hi[rows, :], la_lo[rows, :]], axis=0)))
    pv = proj_seg("pool_value")

    fillers = ["pool_gate", "merge_pool", "gla_gate", "merge_gla"]
    filled = []
    for r in range(nblk):
        rows = slice(r * GLA_BLOCK, (r + 1) * GLA_BLOCK)
        cb = cbs[r]
        b_mid = cb[GLA_BLOCK // 2 - 1:GLA_BLOCK // 2, :]
        b_end = cb[GLA_BLOCK - 1:GLA_BLOCK, :]
        q_mid = q[rows, :] * jnp.exp(cb - b_mid)
        k_mid = k[rows, :] * jnp.exp(b_mid - cb)
        q_in = q_mid.astype(BF16)
        k_in = k_mid.astype(BF16)
        q_start = (q_mid * jnp.exp(b_mid)).astype(BF16)
        k_end = (k_mid * jnp.exp(b_end - b_mid)).astype(BF16)
        block_decay = jnp.exp(b_end)
        heads = range(GLA_HEADS)
        kcs = [slice(hh * GLA_HEAD_K, (hh + 1) * GLA_HEAD_K) for hh in heads]
        vcs = [slice(hh * GLA_HEAD_V, (hh + 1) * GLA_HEAD_V) for hh in heads]
        scores = [lax.dot_general(q_in[:, kcs[hh]], k_in[:, kcs[hh]], _NT, preferred_element_type=F32) for hh in heads]
        vbs = [v[rows, vcs[hh]] for hh in heads]
        updates = [_dot(vbs[hh].T.astype(BF16), k_end[:, kcs[hh]]) for hh in heads]
        for hh in heads:
            state_t = st_ref[hh]
            o = _dot(jnp.where(causal, scores[hh], 0.0).astype(BF16), vbs[hh].astype(BF16))
            o = o + lax.dot_general(q_start[:, kcs[hh]], state_t.astype(BF16), _NT, preferred_element_type=F32)
            o_scr[rows, vcs[hh]] = o
            st_ref[hh] = state_t * block_decay[:, kcs[hh]] + updates[hh]
        if r < len(fillers):
            filled.append(proj_seg(fillers[r]))
    for name in fillers[len(filled):]:
        filled.append(proj_seg(name))
    pg, mgp, gg, mgg = filled

    t_plus_1 = lax.broadcasted_iota(jnp.int32, (ts, 1), 0) + (j * ts + 1)
    mixed = []
    for g, w in enumerate(POOL_WINDOWS):
        cols = slice(g * POOL_GROUP_DIM, (g + 1) * POOL_GROUP_DIM)
        win = jnp.concatenate([halo_ref[:, cols], pv[:, cols]], axis=0)
        step = 1
        while step < min(w, SUBLANES):
            win = win + pltpu.roll(win, step, axis=0)
            step *= 2
        if w > SUBLANES:
            win = win[POOL_HALO:, :] + win[POOL_HALO - SUBLANES:-SUBLANES, :]
        else:
            win = win[POOL_HALO:, :]
        inv_cnt = 1.0 / jnp.minimum(t_plus_1, w).astype(F32)
        pooled = win * inv_cnt - pv[:, cols]
        mixed.append(_dot(pooled.astype(BF16), wgrp_ref[g].astype(BF16)))
    mixed = jnp.concatenate(mixed, axis=1)
    halo_ref[...] = pv[ts - POOL_HALO:, :]
    y_pool = mixed * pscale_ref[...] * _silu(pg)
    merged = _sigmoid(mgp) * _dot(y_pool.astype(BF16), wpo_ref[...])

    o = o_scr[...]
    normed = []
    for hh in range(GLA_HEADS):
        oh = o[:, hh * GLA_HEAD_V:(hh + 1) * GLA_HEAD_V]
        normed.append(oh * lax.rsqrt(jnp.mean(oh * oh, axis=-1, keepdims=True) + EPS) * ghead_ref[...])
    y_gla = (jnp.concatenate(normed, axis=1) * _silu(gg)).astype(BF16)
    gate_gla = _sigmoid(mgg)
    gfin = gfin_ref[...]
    g_out = [_dot(y_gla[hs, :], wgo_ref[...]) for hs in halves]
    for hs, g_half in zip(halves, g_out):
        merged_half = merged[hs, :] + gate_gla[hs, :] * g_half
        y = x_ref[hs, :] + gate * _dot(merged_half.astype(BF16), wo_ref[...])
        out_ref[hs, :] = y * lax.rsqrt(jnp.mean(y * y, axis=-1, keepdims=True) + EPS) * gfin


def _resident(shape):
    return pl.BlockSpec(shape, lambda b, j: (0,) * len(shape), pipeline_mode=pl.Buffered(1))


@functools.partial(jax.jit, static_argnames=("row_tile",))
def _forward(x, c, g_norm, w_ada, b_ada, w_in, w_pool_group, pool_scale, w_alpha_up, b_alpha, g_gla_head,
             w_pool_out, w_gla_out, w_out, g_final, row_tile=ROW_TILE):
    assert g_norm.shape[0] == 1, "single-layer stack"
    bsz, seq, d = x.shape
    ts = row_tile
    assert d == D_MODEL and seq % ts == 0 and ts % (2 * GLA_BLOCK) == 0
    assert w_in.shape == (1, d, IN_WIDTH_REF) and w_ada.shape == (1, d, N_MOD * STAGE)

    row = lambda a: a.reshape(1, -1).astype(F32)
    vmem_operands = [
        c, row(b_ada), row(g_norm), w_pool_group[0], row(pool_scale), w_alpha_up[0], row(b_alpha),
        row(g_gla_head), row(g_final),
    ]
    hbm_operands = [w_in[0].T, w_ada[0], w_pool_out[0], w_gla_out[0], w_out[0]]
    in_specs = [pl.BlockSpec((None, ts, d), lambda b, j: (b, j, 0))]
    in_specs += [_resident(a.shape) for a in vmem_operands]
    in_specs += [pl.BlockSpec(memory_space=pl.ANY) for _ in hbm_operands]

    return pl.pallas_call(
        _layer_kernel,
        out_shape=jax.ShapeDtypeStruct((bsz, seq, d), x.dtype),
        grid=(bsz, seq // ts),
        in_specs=in_specs,
        out_specs=pl.BlockSpec((None, ts, d), lambda b, j: (b, j, 0)),
        scratch_shapes=[pltpu.VMEM((d, IN_WIDTH_PACKED), BF16),
                        pltpu.VMEM((N_SQUARE, d, d), BF16),
                        pltpu.VMEM((bsz, N_MOD * d), F32),
                        pltpu.VMEM((2, STAGE, STAGE), F32),
                        pltpu.SemaphoreType.DMA((2,)),
                        pltpu.VMEM((GLA_HEADS, GLA_HEAD_V, GLA_HEAD_K), F32),
                        pltpu.VMEM((POOL_HALO, d), F32),
                        pltpu.VMEM((ts, GLA_VAL_DIM), F32),
                        pltpu.VMEM((ts, d), BF16)],
        compiler_params=pltpu.CompilerParams(dimension_semantics=("arbitrary", "arbitrary"),
                                             vmem_limit_bytes=VMEM_LIMIT_BYTES),
        name="hybrid_pool_gla_layer",
    )(x, *vmem_operands, *hbm_operands)


def kernel(x, c, g_norm, w_ada, b_ada, w_in, w_pool_group, pool_scale, w_alpha_up, b_alpha, g_gla_head,
           w_pool_out, w_gla_out, w_out, g_final):
    return _forward(x, c, g_norm, w_ada, b_ada, w_in, w_pool_group, pool_scale, w_alpha_up, b_alpha,
                    g_gla_head, w_pool_out, w_gla_out, w_out, g_final)
```

```python
import functools

import jax
import jax.numpy as jnp
from jax import lax
from jax.experimental import pallas as pl
from jax.experimental.pallas import tpu as pltpu

F32 = jnp.float32
BF16 = jnp.bfloat16

D_MODEL = 1024
EPS = 1e-6
POOL_WINDOWS = (2, 4, 8, 16)
POOL_GROUP_DIM = D_MODEL // len(POOL_WINDOWS)
POOL_HALO = 16
GLA_HEADS = 4
GLA_KEY_DIM = D_MODEL // 2
GLA_VAL_DIM = D_MODEL
GLA_HEAD_K = GLA_KEY_DIM // GLA_HEADS
GLA_HEAD_V = GLA_VAL_DIM // GLA_HEADS
GLA_GATE_RANK = 16
GLA_GATE_NORMALIZER = 16.0
GLA_BLOCK = 128
LANES = 128
SUBLANES = 8
assert max(POOL_WINDOWS) <= 2 * SUBLANES <= POOL_HALO
ROW_TILE = 512
STAGE = 1024
VMEM_LIMIT_BYTES = 60 * 1024 * 1024

_IN_SEGMENTS = (("pool_value", D_MODEL), ("pool_gate", D_MODEL), ("q", GLA_KEY_DIM), ("k", GLA_KEY_DIM),
                ("decay_gate", LANES), ("v", GLA_VAL_DIM), ("gla_gate", GLA_VAL_DIM), ("merge_pool", D_MODEL),
                ("merge_gla", D_MODEL))
_IN_COLS = {}
_start = 0
for _name, _width in _IN_SEGMENTS:
    _IN_COLS[_name] = (_start, _start + _width)
    _start += _width
IN_WIDTH_PACKED = _start
_REF_GATE_COL = 2 * D_MODEL + 2 * GLA_KEY_DIM + 2 * GLA_VAL_DIM
IN_WIDTH_REF = _REF_GATE_COL + GLA_GATE_RANK + 2 * D_MODEL
_IN_CHUNKS = ((0, _IN_COLS["pool_value"][0]), (D_MODEL, _IN_COLS["pool_gate"][0]), (2 * D_MODEL, _IN_COLS["q"][0]),
              (2 * D_MODEL + 2 * GLA_KEY_DIM, _IN_COLS["v"][0]),
              (2 * D_MODEL + 2 * GLA_KEY_DIM + GLA_VAL_DIM, _IN_COLS["gla_gate"][0]),
              (_REF_GATE_COL + GLA_GATE_RANK, _IN_COLS["merge_pool"][0]),
              (_REF_GATE_COL + GLA_GATE_RANK + D_MODEL, _IN_COLS["merge_gla"][0]))
assert _IN_COLS["k"][0] == _IN_COLS["q"][0] + GLA_KEY_DIM and 2 * GLA_KEY_DIM == STAGE == D_MODEL
N_SQUARE = 3
N_MOD = 3

_NT = (((1,), (1,)), ((), ()))


def _dot(a, b):
    return jnp.dot(a, b, preferred_element_type=F32)


def _sigmoid(x):
    return 0.5 * jnp.tanh(0.5 * x) + 0.5


def _silu(x):
    half = 0.5 * x
    return half * jnp.tanh(half) + half


def _log_sigmoid(x):
    return jnp.minimum(x, 0.0) - jnp.log1p(jnp.exp(-jnp.abs(x)))


def _prepare_weights(c_ref, bada_ref, wint_hbm, wada_hbm, wsq_hbms, win_ref, wsq_ref, mod_ref, stage_ref, sem_ref):
    gate_chunk = len(_IN_CHUNKS)
    n_chunks = gate_chunk + 1 + N_SQUARE + N_MOD

    def copy(i):
        slot = i % 2
        if i < gate_chunk:
            src = wint_hbm.at[pl.ds(_IN_CHUNKS[i][0], STAGE), :]
            dst = stage_ref.at[slot]
        elif i == gate_chunk:
            src = wint_hbm.at[pl.ds(_REF_GATE_COL, LANES), :]
            dst = stage_ref.at[slot, pl.ds(0, LANES), :]
        elif i < gate_chunk + 1 + N_SQUARE:
            src = wsq_hbms[i - gate_chunk - 1]
            dst = stage_ref.at[slot]
        else:
            src = wada_hbm.at[:, pl.ds((i - gate_chunk - 1 - N_SQUARE) * STAGE, STAGE)]
            dst = stage_ref.at[slot]
        return pltpu.make_async_copy(src, dst, sem_ref.at[slot])

    silu_c = _silu(c_ref[...]).astype(BF16)
    copy(0).start()
    for i in range(n_chunks):
        if i + 1 < n_chunks:
            copy(i + 1).start()
        copy(i).wait()
        slot = i % 2
        if i < gate_chunk:
            dst0 = _IN_CHUNKS[i][1]
            win_ref[:, dst0:dst0 + STAGE] = stage_ref[slot].T.astype(BF16)
        elif i == gate_chunk:
            tile = stage_ref[slot, 0:LANES, :].T
            lane = lax.broadcasted_iota(jnp.int32, tile.shape, 1)
            g0, g1 = _IN_COLS["decay_gate"]
            win_ref[:, g0:g1] = jnp.where(lane < GLA_GATE_RANK, tile, 0.0).astype(BF16)
        elif i < gate_chunk + 1 + N_SQUARE:
            wsq_ref[i - gate_chunk - 1] = stage_ref[slot].astype(BF16)
        else:
            c0 = (i - gate_chunk - 1 - N_SQUARE) * STAGE
            mod_ref[:, c0:c0 + STAGE] = _dot(silu_c, stage_ref[slot].astype(BF16)) + bada_ref[:, c0:c0 + STAGE]


def _layer_kernel(x_ref, c_ref, bada_ref, gnorm_ref, wgrp_ref, pscale_ref, wup_ref, balpha_ref, ghead_ref, gfin_ref,
                  wint_hbm, wada_hbm, wpo_hbm, wgo_hbm, wo_hbm, out_ref,
                  win_ref, wsq_ref, mod_ref, stage_ref, sem_ref, st_ref, halo_ref, o_scr, hb_ref):
    b = pl.program_id(0)
    j = pl.program_id(1)
    ts = x_ref.shape[0]
    d = D_MODEL

    @pl.when(jnp.logical_and(b == 0, j == 0))
    def _():
        _prepare_weights(c_ref, bada_ref, wint_hbm, wada_hbm, (wpo_hbm, wgo_hbm, wo_hbm), win_ref, wsq_ref, mod_ref,
                         stage_ref, sem_ref)

    @pl.when(j == 0)
    def _():
        st_ref[...] = jnp.zeros_like(st_ref)
        halo_ref[...] = jnp.zeros_like(halo_ref)

    wpo_ref, wgo_ref, wo_ref = wsq_ref.at[0], wsq_ref.at[1], wsq_ref.at[2]
    x = x_ref[...]
    mod = mod_ref[pl.ds(b, 1), :]
    shift, scale, gate = mod[:, 0:d], mod[:, d:2 * d], mod[:, 2 * d:3 * d]
    h = x * lax.rsqrt(jnp.mean(x * x, axis=-1, keepdims=True) + EPS) * gnorm_ref[...]
    h = h * (1.0 + scale) + shift
    hb_ref[...] = h.astype(BF16)

    def proj(c0, c1, rows=slice(None)):
        return _dot(hb_ref[rows, :], win_ref[:, c0:c1])

    def proj_seg(name):
        return proj(*_IN_COLS[name])

    half = ts // 2
    halves = [slice(0, half), slice(half, ts)]
    k0, g1 = _IN_COLS["k"][0], _IN_COLS["decay_gate"][1]
    ka = proj(k0, g1)
    k = ka[:, 0:GLA_KEY_DIM]
    a_low = ka[:, GLA_KEY_DIM:GLA_KEY_DIM + LANES]
    wup = jnp.concatenate([wup_ref[...].astype(BF16), jnp.zeros((LANES - GLA_GATE_RANK, GLA_KEY_DIM), BF16)], axis=0)
    log_a = _log_sigmoid(_dot(a_low.astype(BF16), wup) + balpha_ref[...]) * (1.0 / GLA_GATE_NORMALIZER)
    la_hi = log_a.astype(BF16)
    la_lo = (log_a - la_hi.astype(F32)).astype(BF16)
    q = proj_seg("q") * (GLA_HEAD_K ** -0.5)
    v = proj_seg("v")
    rr = lax.broadcasted_iota(jnp.int32, (GLA_BLOCK, GLA_BLOCK), 0)
    cc = lax.broadcasted_iota(jnp.int32, (GLA_BLOCK, GLA_BLOCK), 1)
    causal = rr >= cc
    tri = jnp.where(causal, 1.0, 0.0).astype(BF16)
    tri2 = jnp.concatenate([tri, tri], axis=1)
    nblk = ts // GLA_BLOCK
    cbs = []
    for r in range(nblk):
        rows = slice(r * GLA_BLOCK, (r + 1) * GLA_BLOCK)
        cbs.append(_dot(tri2, jnp.concatenate([la_hi[rows, :], la_lo[rows, :]], axis=0)))
    pv = proj_seg("pool_value")

    fillers = ["pool_gate", "merge_pool", "gla_gate", "merge_gla"]
    filled = []
    for r in range(nblk):
        rows = slice(r * GLA_BLOCK, (r + 1) * GLA_BLOCK)
        cb = cbs[r]
        b_mid = cb[GLA_BLOCK // 2 - 1:GLA_BLOCK // 2, :]
        b_end = cb[GLA_BLOCK - 1:GLA_BLOCK, :]
        q_mid = q[rows, :] * jnp.exp(cb - b_mid)
        k_mid = k[rows, :] * jnp.exp(b_mid - cb)
        q_in = q_mid.astype(BF16)
        k_in = k_mid.astype(BF16)
        q_start = (q_mid * jnp.exp(b_mid)).astype(BF16)
        k_end = (k_mid * jnp.exp(b_end - b_mid)).astype(BF16)
        block_decay = jnp.exp(b_end)
        heads = range(GLA_HEADS)
        kcs = [slice(hh * GLA_HEAD_K, (hh + 1) * GLA_HEAD_K) for hh in heads]
        vcs = [slice(hh * GLA_HEAD_V, (hh + 1) * GLA_HEAD_V) for hh in heads]
        scores = [lax.dot_general(q_in[:, kcs[hh]], k_in[:, kcs[hh]], _NT, preferred_element_type=F32) for hh in heads]
        vbs = [v[rows, vcs[hh]] for hh in heads]
        updates = [_dot(vbs[hh].T.astype(BF16), k_end[:, kcs[hh]]) for hh in heads]
        for hh in heads:
            state_t = st_ref[hh]
            o = _dot(jnp.where(causal, scores[hh], 0.0).astype(BF16), vbs[hh].astype(BF16))
            o = o + lax.dot_general(q_start[:, kcs[hh]], state_t.astype(BF16), _NT, preferred_element_type=F32)
            o_scr[rows, vcs[hh]] = o
            st_ref[hh] = state_t * block_decay[:, kcs[hh]] + updates[hh]
        if r < len(fillers):
            filled.append(proj_seg(fillers[r]))
    for name in fillers[len(filled):]:
        filled.append(proj_seg(name))
    pg, mgp, gg, mgg = filled
    pool_gate = _silu(pg).astype(BF16)
    merge_pool_gate = _sigmoid(mgp).astype(BF16)
    gla_gate = _silu(gg).astype(BF16)
    gate_gla = _sigmoid(mgg).astype(BF16)

    t_plus_1 = lax.broadcasted_iota(jnp.int32, (ts, 1), 0) + (j * ts + 1)
    mixed = []
    for g, w in enumerate(POOL_WINDOWS):
        cols = slice(g * POOL_GROUP_DIM, (g + 1) * POOL_GROUP_DIM)
        win = jnp.concatenate([halo_ref[:, cols], pv[:, cols]], axis=0)
        step = 1
        while step < min(w, SUBLANES):
            win = win + pltpu.roll(win, step, axis=0)
            step *= 2
        if w > SUBLANES:
            win = win[POOL_HALO:, :] + win[POOL_HALO - SUBLANES:-SUBLANES, :]
        else:
            win = win[POOL_HALO:, :]
        inv_cnt = 1.0 / jnp.minimum(t_plus_1, w).astype(F32)
        pooled = win * inv_cnt - pv[:, cols]
        mixed.append(_dot(pooled.astype(BF16), wgrp_ref[g].astype(BF16)))
    mixed = jnp.concatenate(mixed, axis=1)
    halo_ref[...] = pv[ts - POOL_HALO:, :]
    y_pool = mixed * pscale_ref[...] * pool_gate
    merged = merge_pool_gate * _dot(y_pool.astype(BF16), wpo_ref[...])

    o = o_scr[...]
    normed = []
    for hh in range(GLA_HEADS):
        oh = o[:, hh * GLA_HEAD_V:(hh + 1) * GLA_HEAD_V]
        normed.append(oh * lax.rsqrt(jnp.mean(oh * oh, axis=-1, keepdims=True) + EPS) * ghead_ref[...])
    y_gla = (jnp.concatenate(normed, axis=1) * gla_gate).astype(BF16)
    gfin = gfin_ref[...]
    g_out = [_dot(y_gla[hs, :], wgo_ref[...]) for hs in halves]
    for hs, g_half in zip(halves, g_out):
        merged_half = merged[hs, :] + gate_gla[hs, :] * g_half
        y = x_ref[hs, :] + gate * _dot(merged_half.astype(BF16), wo_ref[...])
        out_ref[hs, :] = y * lax.rsqrt(jnp.mean(y * y, axis=-1, keepdims=True) + EPS) * gfin


def _resident(shape):
    return pl.BlockSpec(shape, lambda b, j: (0,) * len(shape), pipeline_mode=pl.Buffered(1))


@functools.partial(jax.jit, static_argnames=("row_tile",))
def _forward(x, c, g_norm, w_ada, b_ada, w_in, w_pool_group, pool_scale, w_alpha_up, b_alpha, g_gla_head,
             w_pool_out, w_gla_out, w_out, g_final, row_tile=ROW_TILE):
    assert g_norm.shape[0] == 1, "single-layer stack"
    bsz, seq, d = x.shape
    ts = row_tile
    assert d == D_MODEL and seq % ts == 0 and ts % (2 * GLA_BLOCK) == 0
    assert w_in.shape == (1, d, IN_WIDTH_REF) and w_ada.shape == (1, d, N_MOD * STAGE)

    row = lambda a: a.reshape(1, -1).astype(F32)
    vmem_operands = [
        c, row(b_ada), row(g_norm), w_pool_group[0], row(pool_scale), w_alpha_up[0], row(b_alpha),
        row(g_gla_head), row(g_final),
    ]
    hbm_operands = [w_in[0].T, w_ada[0], w_pool_out[0], w_gla_out[0], w_out[0]]
    in_specs = [pl.BlockSpec((None, ts, d), lambda b, j: (b, j, 0))]
    in_specs += [_resident(a.shape) for a in vmem_operands]
    in_specs += [pl.BlockSpec(memory_space=pl.ANY) for _ in hbm_operands]

    return pl.pallas_call(
        _layer_kernel,
        out_shape=jax.ShapeDtypeStruct((bsz, seq, d), x.dtype),
        grid=(bsz, seq // ts),
        in_specs=in_specs,
        out_specs=pl.BlockSpec((None, ts, d), lambda b, j: (b, j, 0)),
        scratch_shapes=[pltpu.VMEM((d, IN_WIDTH_PACKED), BF16),
                        pltpu.VMEM((N_SQUARE, d, d), BF16),
                        pltpu.VMEM((bsz, N_MOD * d), F32),
                        pltpu.VMEM((2, STAGE, STAGE), F32),
                        pltpu.SemaphoreType.DMA((2,)),
                        pltpu.VMEM((GLA_HEADS, GLA_HEAD_V, GLA_HEAD_K), F32),
                        pltpu.VMEM((POOL_HALO, d), F32),
                        pltpu.VMEM((ts, GLA_VAL_DIM), F32),
                        pltpu.VMEM((ts, d), BF16)],
        compiler_params=pltpu.CompilerParams(dimension_semantics=("arbitrary", "arbitrary"),
                                             vmem_limit_bytes=VMEM_LIMIT_BYTES),
        name="hybrid_pool_gla_layer",
    )(x, *vmem_operands, *hbm_operands)


def kernel(x, c, g_norm, w_ada, b_ada, w_in, w_pool_group, pool_scale, w_alpha_up, b_alpha, g_gla_head,
           w_pool_out, w_gla_out, w_out, g_final):
    return _forward(x, c, g_norm, w_ada, b_ada, w_in, w_pool_group, pool_scale, w_alpha_up, b_alpha,
                    g_gla_head, w_pool_out, w_gla_out, w_out, g_final)
```

```python
import functools

import jax
import jax.numpy as jnp
from jax import lax
from jax.experimental import pallas as pl
from jax.experimental.pallas import tpu as pltpu

F32 = jnp.float32
BF16 = jnp.bfloat16

D_MODEL = 1024
EPS = 1e-6
POOL_WINDOWS = (2, 4, 8, 16)
POOL_GROUP_DIM = D_MODEL // len(POOL_WINDOWS)
POOL_HALO = 16
GLA_HEADS = 4
GLA_KEY_DIM = D_MODEL // 2
GLA_VAL_DIM = D_MODEL
GLA_HEAD_K = GLA_KEY_DIM // GLA_HEADS
GLA_HEAD_V = GLA_VAL_DIM // GLA_HEADS
GLA_GATE_RANK = 16
GLA_GATE_NORMALIZER = 16.0
GLA_BLOCK = 128
LANES = 128
SUBLANES = 8
assert max(POOL_WINDOWS) <= 2 * SUBLANES <= POOL_HALO
ROW_TILE = 512
STAGE = 1024
VMEM_LIMIT_BYTES = 60 * 1024 * 1024

_IN_SEGMENTS = (("pool_value", D_MODEL), ("pool_gate", D_MODEL), ("q", GLA_KEY_DIM), ("k", GLA_KEY_DIM),
                ("decay_gate", LANES), ("v", GLA_VAL_DIM), ("gla_gate", GLA_VAL_DIM), ("merge_pool", D_MODEL),
                ("merge_gla", D_MODEL))
_IN_COLS = {}
_start = 0
for _name, _width in _IN_SEGMENTS:
    _IN_COLS[_name] = (_start, _start + _width)
    _start += _width
IN_WIDTH_PACKED = _start
_REF_GATE_COL = 2 * D_MODEL + 2 * GLA_KEY_DIM + 2 * GLA_VAL_DIM
IN_WIDTH_REF = _REF_GATE_COL + GLA_GATE_RANK + 2 * D_MODEL
_IN_CHUNKS = ((0, _IN_COLS["pool_value"][0]), (D_MODEL, _IN_COLS["pool_gate"][0]), (2 * D_MODEL, _IN_COLS["q"][0]),
              (2 * D_MODEL + 2 * GLA_KEY_DIM, _IN_COLS["v"][0]),
              (2 * D_MODEL + 2 * GLA_KEY_DIM + GLA_VAL_DIM, _IN_COLS["gla_gate"][0]),
              (_REF_GATE_COL + GLA_GATE_RANK, _IN_COLS["merge_pool"][0]),
              (_REF_GATE_COL + GLA_GATE_RANK + D_MODEL, _IN_COLS["merge_gla"][0]))
assert _IN_COLS["k"][0] == _IN_COLS["q"][0] + GLA_KEY_DIM and 2 * GLA_KEY_DIM == STAGE == D_MODEL
N_SQUARE = 3
N_MOD = 3

_NT = (((1,), (1,)), ((), ()))


def _dot(a, b):
    return jnp.dot(a, b, preferred_element_type=F32)


def _sigmoid(x):
    return 0.5 * jnp.tanh(0.5 * x) + 0.5


def _silu(x):
    half = 0.5 * x
    return half * jnp.tanh(half) + half


def _log_sigmoid(x):
    return jnp.minimum(x, 0.0) - jnp.log1p(jnp.exp(-jnp.abs(x)))


def _prepare_weights(c_ref, bada_ref, wint_hbm, wada_hbm, wsq_hbms, win_ref, wsq_ref, mod_ref, stage_ref, sem_ref):
    gate_chunk = len(_IN_CHUNKS)
    n_chunks = gate_chunk + 1 + N_SQUARE + N_MOD

    def copy(i):
        slot = i % 2
        if i < gate_chunk:
            src = wint_hbm.at[pl.ds(_IN_CHUNKS[i][0], STAGE), :]
            dst = stage_ref.at[slot]
        elif i == gate_chunk:
            src = wint_hbm.at[pl.ds(_REF_GATE_COL, LANES), :]
            dst = stage_ref.at[slot, pl.ds(0, LANES), :]
        elif i < gate_chunk + 1 + N_SQUARE:
            src = wsq_hbms[i - gate_chunk - 1]
            dst = stage_ref.at[slot]
        else:
            src = wada_hbm.at[:, pl.ds((i - gate_chunk - 1 - N_SQUARE) * STAGE, STAGE)]
            dst = stage_ref.at[slot]
        return pltpu.make_async_copy(src, dst, sem_ref.at[slot])

    silu_c = _silu(c_ref[...]).astype(BF16)
    copy(0).start()
    for i in range(n_chunks):
        if i + 1 < n_chunks:
            copy(i + 1).start()
        copy(i).wait()
        slot = i % 2
        if i < gate_chunk:
            dst0 = _IN_CHUNKS[i][1]
            win_ref[:, dst0:dst0 + STAGE] = stage_ref[slot].T.astype(BF16)
        elif i == gate_chunk:
            tile = stage_ref[slot, 0:LANES, :].T
            lane = lax.broadcasted_iota(jnp.int32, tile.shape, 1)
            g0, g1 = _IN_COLS["decay_gate"]
            win_ref[:, g0:g1] = jnp.where(lane < GLA_GATE_RANK, tile, 0.0).astype(BF16)
        elif i < gate_chunk + 1 + N_SQUARE:
            wsq_ref[i - gate_chunk - 1] = stage_ref[slot].astype(BF16)
        else:
            c0 = (i - gate_chunk - 1 - N_SQUARE) * STAGE
            mod_ref[:, c0:c0 + STAGE] = _dot(silu_c, stage_ref[slot].astype(BF16)) + bada_ref[:, c0:c0 + STAGE]


def _layer_kernel(x_ref, c_ref, bada_ref, gnorm_ref, wgrp_ref, pscale_ref, wup_ref, balpha_ref, ghead_ref, gfin_ref,
                  wint_hbm, wada_hbm, wpo_hbm, wgo_hbm, wo_hbm, out_ref,
                  win_ref, wsq_ref, mod_ref, stage_ref, sem_ref, st_ref, halo_ref, o_scr, hb_ref):
    b = pl.program_id(0)
    j = pl.program_id(1)
    ts = x_ref.shape[0]
    d = D_MODEL

    @pl.when(jnp.logical_and(b == 0, j == 0))
    def _():
        _prepare_weights(c_ref, bada_ref, wint_hbm, wada_hbm, (wpo_hbm, wgo_hbm, wo_hbm), win_ref, wsq_ref, mod_ref,
                         stage_ref, sem_ref)

    @pl.when(j == 0)
    def _():
        st_ref[...] = jnp.zeros_like(st_ref)
        halo_ref[...] = jnp.zeros_like(halo_ref)

    wpo_ref, wgo_ref, wo_ref = wsq_ref.at[0], wsq_ref.at[1], wsq_ref.at[2]
    x = x_ref[...]
    mod = mod_ref[pl.ds(b, 1), :]
    shift, scale, gate = mod[:, 0:d], mod[:, d:2 * d], mod[:, 2 * d:3 * d]
    h = x * lax.rsqrt(jnp.mean(x * x, axis=-1, keepdims=True) + EPS) * gnorm_ref[...]
    h = h * (1.0 + scale) + shift
    hb_ref[...] = h.astype(BF16)

    def proj(c0, c1, rows=slice(None)):
        return _dot(hb_ref[rows, :], win_ref[:, c0:c1])

    def proj_seg(name):
        return proj(*_IN_COLS[name])

    half = ts // 2
    halves = [slice(0, half), slice(half, ts)]
    k0, g1 = _IN_COLS["k"][0], _IN_COLS["decay_gate"][1]
    ka = proj(k0, g1)
    k = ka[:, 0:GLA_KEY_DIM]
    a_low = ka[:, GLA_KEY_DIM:GLA_KEY_DIM + LANES]
    wup = jnp.concatenate([wup_ref[...].astype(BF16), jnp.zeros((LANES - GLA_GATE_RANK, GLA_KEY_DIM), BF16)], axis=0)
    log_a = _log_sigmoid(_dot(a_low.astype(BF16), wup) + balpha_ref[...]) * (1.0 / GLA_GATE_NORMALIZER)
    la_hi = log_a.astype(BF16)
    la_lo = (log_a - la_hi.astype(F32)).astype(BF16)
    q = proj_seg("q") * (GLA_HEAD_K ** -0.5)
    v = proj_seg("v")
    rr = lax.broadcasted_iota(jnp.int32, (GLA_BLOCK, GLA_BLOCK), 0)
    cc = lax.broadcasted_iota(jnp.int32, (GLA_BLOCK, GLA_BLOCK), 1)
    causal = rr >= cc
    tri = jnp.where(causal, 1.0, 0.0).astype(BF16)
    tri2 = jnp.concatenate([tri, tri], axis=1)
    nblk = ts // GLA_BLOCK
    cbs = []
    for r in range(nblk):
        rows = slice(r * GLA_BLOCK, (r + 1) * GLA_BLOCK)
        cbs.append(_dot(tri2, jnp.concatenate([la_hi[rows, :], la_lo[rows, :]], axis=0)))
    pv = proj_seg("pool_value")

    fillers = ["pool_gate", "merge_pool", "gla_gate", "merge_gla"]
    filled = []
    for r in range(nblk):
        rows = slice(r * GLA_BLOCK, (r + 1) * GLA_BLOCK)
        cb = cbs[r]
        b_mid = cb[GLA_BLOCK // 2 - 1:GLA_BLOCK // 2, :]
        b_end = cb[GLA_BLOCK - 1:GLA_BLOCK, :]
        q_mid = q[rows, :] * jnp.exp(cb - b_mid)
        k_mid = k[rows, :] * jnp.exp(b_mid - cb)
        q_in = q_mid.astype(BF16)
        k_in = k_mid.astype(BF16)
        q_start = (q_mid * jnp.exp(b_mid)).astype(BF16)
        k_end = (k_mid * jnp.exp(b_end - b_mid)).astype(BF16)
        block_decay = jnp.exp(b_end)
        heads = range(GLA_HEADS)
        kcs = [slice(hh * GLA_HEAD_K, (hh + 1) * GLA_HEAD_K) for hh in heads]
        vcs = [slice(hh * GLA_HEAD_V, (hh + 1) * GLA_HEAD_V) for hh in heads]
        scores = [lax.dot_general(q_in[:, kcs[hh]], k_in[:, kcs[hh]], _NT, preferred_element_type=F32) for hh in heads]
        vbs = [v[rows, vcs[hh]] for hh in heads]
        updates = [_dot(vbs[hh].T.astype(BF16), k_end[:, kcs[hh]]) for hh in heads]
        for hh in heads:
            state_t = st_ref[hh]
            o = _dot(jnp.where(causal, scores[hh], 0.0).astype(BF16), vbs[hh].astype(BF16))
            o = o + lax.dot_general(q_start[:, kcs[hh]], state_t.astype(BF16), _NT, preferred_element_type=F32)
            o_scr[rows, vcs[hh]] = o
            st_ref[hh] = state_t * block_decay[:, kcs[hh]] + updates[hh]
        if r < len(fillers):
            filled.append(proj_seg(fillers[r]))
    for name in fillers[len(filled):]:
        filled.append(proj_seg(name))
    pg, mgp, gg, mgg = filled
    pool_gate = _silu(pg).astype(BF16)
    merge_pool_gate = _sigmoid(mgp).astype(BF16)
    gla_gate = _silu(gg).astype(BF16)
    gate_gla = _sigmoid(mgg).astype(BF16)

    head_t_plus_1 = lax.broadcasted_iota(jnp.int32, (POOL_HALO, 1), 0) + (j * ts + 1)
    mixed = []
    for g, w in enumerate(POOL_WINDOWS):
        cols = slice(g * POOL_GROUP_DIM, (g + 1) * POOL_GROUP_DIM)
        win = jnp.concatenate([halo_ref[:, cols], pv[:, cols]], axis=0)
        step = 1
        while step < min(w, SUBLANES):
            win = win + pltpu.roll(win, step, axis=0)
            step *= 2
        if w > SUBLANES:
            win = win[POOL_HALO:, :] + win[POOL_HALO - SUBLANES:-SUBLANES, :]
        else:
            win = win[POOL_HALO:, :]
        head_inv_cnt = 1.0 / jnp.minimum(head_t_plus_1, w).astype(F32)
        mean = jnp.concatenate([win[:POOL_HALO, :] * head_inv_cnt, win[POOL_HALO:, :] * (1.0 / w)], axis=0)
        pooled = mean - pv[:, cols]
        mixed.append(_dot(pooled.astype(BF16), wgrp_ref[g].astype(BF16)))
    mixed = jnp.concatenate(mixed, axis=1)
    halo_ref[...] = pv[ts - POOL_HALO:, :]
    y_pool = mixed * pscale_ref[...] * pool_gate
    merged = merge_pool_gate * _dot(y_pool.astype(BF16), wpo_ref[...])

    o = o_scr[...]
    normed = []
    for hh in range(GLA_HEADS):
        oh = o[:, hh * GLA_HEAD_V:(hh + 1) * GLA_HEAD_V]
        normed.append(oh * lax.rsqrt(jnp.mean(oh * oh, axis=-1, keepdims=True) + EPS) * ghead_ref[...])
    y_gla = (jnp.concatenate(normed, axis=1) * gla_gate).astype(BF16)
    gfin = gfin_ref[...]
    g_out = [_dot(y_gla[hs, :], wgo_ref[...]) for hs in halves]
    for hs, g_half in zip(halves, g_out):
        merged_half = merged[hs, :] + gate_gla[hs, :] * g_half
        y = x_ref[hs, :] + gate * _dot(merged_half.astype(BF16), wo_ref[...])
        out_ref[hs, :] = y * lax.rsqrt(jnp.mean(y * y, axis=-1, keepdims=True) + EPS) * gfin


def _resident(shape):
    return pl.BlockSpec(shape, lambda b, j: (0,) * len(shape), pipeline_mode=pl.Buffered(1))


@functools.partial(jax.jit, static_argnames=("row_tile",))
def _forward(x, c, g_norm, w_ada, b_ada, w_in, w_pool_group, pool_scale, w_alpha_up, b_alpha, g_gla_head,
             w_pool_out, w_gla_out, w_out, g_final, row_tile=ROW_TILE):
    assert g_norm.shape[0] == 1, "single-layer stack"
    bsz, seq, d = x.shape
    ts = row_tile
    assert d == D_MODEL and seq % ts == 0 and ts % (2 * GLA_BLOCK) == 0
    assert w_in.shape == (1, d, IN_WIDTH_REF) and w_ada.shape == (1, d, N_MOD * STAGE)

    row = lambda a: a.reshape(1, -1).astype(F32)
    vmem_operands = [
        c, row(b_ada), row(g_norm), w_pool_group[0], row(pool_scale), w_alpha_up[0], row(b_alpha),
        row(g_gla_head), row(g_final),
    ]
    hbm_operands = [w_in[0].T, w_ada[0], w_pool_out[0], w_gla_out[0], w_out[0]]
    in_specs = [pl.BlockSpec((None, ts, d), lambda b, j: (b, j, 0))]
    in_specs += [_resident(a.shape) for a in vmem_operands]
    in_specs += [pl.BlockSpec(memory_space=pl.ANY) for _ in hbm_operands]

    return pl.pallas_call(
        _layer_kernel,
        out_shape=jax.ShapeDtypeStruct((bsz, seq, d), x.dtype),
        grid=(bsz, seq // ts),
        in_specs=in_specs,
        out_specs=pl.BlockSpec((None, ts, d), lambda b, j: (b, j, 0)),
        scratch_shapes=[pltpu.VMEM((d, IN_WIDTH_PACKED), BF16),
                        pltpu.VMEM((N_SQUARE, d, d), BF16),
                        pltpu.VMEM((bsz, N_MOD * d), F32),
                        pltpu.VMEM((2, STAGE, STAGE), F32),
                        pltpu.SemaphoreType.DMA((2,)),
                        pltpu.VMEM((GLA_HEADS, GLA_HEAD_V, GLA_HEAD_K), F32),
                        pltpu.VMEM((POOL_HALO, d), F32),
                        pltpu.VMEM((ts, GLA_VAL_DIM), F32),
                        pltpu.VMEM((ts, d), BF16)],
        compiler_params=pltpu.CompilerParams(dimension_semantics=("arbitrary", "arbitrary"),
                                             vmem_limit_bytes=VMEM_LIMIT_BYTES),
        name="hybrid_pool_gla_layer",
    )(x, *vmem_operands, *hbm_operands)


def kernel(x, c, g_norm, w_ada, b_ada, w_in, w_pool_group, pool_scale, w_alpha_up, b_alpha, g_gla_head,
           w_pool_out, w_gla_out, w_out, g_final):
    return _forward(x, c, g_norm, w_ada, b_ada, w_in, w_pool_group, pool_scale, w_alpha_up, b_alpha,
                    g_gla_head, w_pool_out, w_gla_out, w_out, g_final)
```

```python
import functools
import math

import jax
import jax.numpy as jnp
from jax import lax
from jax.experimental import pallas as pl
from jax.experimental.pallas import tpu as pltpu

F32 = jnp.float32
BF16 = jnp.bfloat16

D_MODEL = 1024
EPS = 1e-6
POOL_WINDOWS = (2, 4, 8, 16)
POOL_GROUP_DIM = D_MODEL // len(POOL_WINDOWS)
POOL_HALO = 16
GLA_HEADS = 4
GLA_KEY_DIM = D_MODEL // 2
GLA_VAL_DIM = D_MODEL
GLA_HEAD_K = GLA_KEY_DIM // GLA_HEADS
GLA_HEAD_V = GLA_VAL_DIM // GLA_HEADS
GLA_GATE_RANK = 16
GLA_GATE_NORMALIZER = 16.0
GLA_BLOCK = 128
LANES = 128
SUBLANES = 8
assert max(POOL_WINDOWS) <= 2 * SUBLANES <= POOL_HALO
ROW_TILE = 512
STAGE = 1024
VMEM_LIMIT_BYTES = 60 * 1024 * 1024

_IN_SEGMENTS = (("pool_value", D_MODEL), ("pool_gate", D_MODEL), ("q", GLA_KEY_DIM), ("k", GLA_KEY_DIM),
                ("decay_gate", LANES), ("v", GLA_VAL_DIM), ("gla_gate", GLA_VAL_DIM), ("merge_pool", D_MODEL),
                ("merge_gla", D_MODEL))
_IN_COLS = {}
_start = 0
for _name, _width in _IN_SEGMENTS:
    _IN_COLS[_name] = (_start, _start + _width)
    _start += _width
IN_WIDTH_PACKED = _start
_REF_GATE_COL = 2 * D_MODEL + 2 * GLA_KEY_DIM + 2 * GLA_VAL_DIM
IN_WIDTH_REF = _REF_GATE_COL + GLA_GATE_RANK + 2 * D_MODEL
_IN_CHUNKS = ((0, _IN_COLS["pool_value"][0]), (D_MODEL, _IN_COLS["pool_gate"][0]), (2 * D_MODEL, _IN_COLS["q"][0]),
              (2 * D_MODEL + 2 * GLA_KEY_DIM, _IN_COLS["v"][0]),
              (2 * D_MODEL + 2 * GLA_KEY_DIM + GLA_VAL_DIM, _IN_COLS["gla_gate"][0]),
              (_REF_GATE_COL + GLA_GATE_RANK, _IN_COLS["merge_pool"][0]),
              (_REF_GATE_COL + GLA_GATE_RANK + D_MODEL, _IN_COLS["merge_gla"][0]))
assert _IN_COLS["k"][0] == _IN_COLS["q"][0] + GLA_KEY_DIM and 2 * GLA_KEY_DIM == STAGE == D_MODEL
N_SQUARE = 3
N_MOD = 3

_NT = (((1,), (1,)), ((), ()))


def _dot(a, b):
    return jnp.dot(a, b, preferred_element_type=F32)


def _sigmoid(x):
    return 0.5 * jnp.tanh(0.5 * x) + 0.5


def _silu(x):
    half = 0.5 * x
    return half * jnp.tanh(half) + half


def _log_sigmoid(x):
    return jnp.minimum(x, 0.0) - jnp.log1p(jnp.exp(-jnp.abs(x)))


def _prepare_weights(c_ref, bada_ref, wint_hbm, wada_hbm, wsq_hbms, win_ref, wsq_ref, mod_ref, stage_ref, sem_ref):
    gate_chunk = len(_IN_CHUNKS)
    n_chunks = gate_chunk + 1 + N_SQUARE + N_MOD

    def copy(i):
        slot = i % 2
        if i < gate_chunk:
            src = wint_hbm.at[pl.ds(_IN_CHUNKS[i][0], STAGE), :]
            dst = stage_ref.at[slot]
        elif i == gate_chunk:
            src = wint_hbm.at[pl.ds(_REF_GATE_COL, LANES), :]
            dst = stage_ref.at[slot, pl.ds(0, LANES), :]
        elif i < gate_chunk + 1 + N_SQUARE:
            src = wsq_hbms[i - gate_chunk - 1]
            dst = stage_ref.at[slot]
        else:
            src = wada_hbm.at[:, pl.ds((i - gate_chunk - 1 - N_SQUARE) * STAGE, STAGE)]
            dst = stage_ref.at[slot]
        return pltpu.make_async_copy(src, dst, sem_ref.at[slot])

    silu_c = _silu(c_ref[...]).astype(BF16)
    copy(0).start()
    for i in range(n_chunks):
        if i + 1 < n_chunks:
            copy(i + 1).start()
        copy(i).wait()
        slot = i % 2
        if i < gate_chunk:
            dst0 = _IN_CHUNKS[i][1]
            win_ref[:, dst0:dst0 + STAGE] = stage_ref[slot].T.astype(BF16)
        elif i == gate_chunk:
            tile = stage_ref[slot, 0:LANES, :].T
            lane = lax.broadcasted_iota(jnp.int32, tile.shape, 1)
            g0, g1 = _IN_COLS["decay_gate"]
            win_ref[:, g0:g1] = jnp.where(lane < GLA_GATE_RANK, tile, 0.0).astype(BF16)
        elif i < gate_chunk + 1 + N_SQUARE:
            wsq_ref[i - gate_chunk - 1] = stage_ref[slot].astype(BF16)
        else:
            c0 = (i - gate_chunk - 1 - N_SQUARE) * STAGE
            mod_ref[:, c0:c0 + STAGE] = _dot(silu_c, stage_ref[slot].astype(BF16)) + bada_ref[:, c0:c0 + STAGE]


def _layer_kernel(x_ref, c_ref, bada_ref, gnorm_ref, wgrp_ref, pscale_ref, wup_ref, balpha_ref, ghead_ref, gfin_ref,
                  wint_hbm, wada_hbm, wpo_hbm, wgo_hbm, wo_hbm, out_ref,
                  win_ref, wsq_ref, mod_ref, stage_ref, sem_ref, st_ref, halo_ref, o_scr, hb_ref):
    b = pl.program_id(0)
    j = pl.program_id(1)
    ts = x_ref.shape[0]
    d = D_MODEL

    @pl.when(jnp.logical_and(b == 0, j == 0))
    def _():
        _prepare_weights(c_ref, bada_ref, wint_hbm, wada_hbm, (wpo_hbm, wgo_hbm, wo_hbm), win_ref, wsq_ref, mod_ref,
                         stage_ref, sem_ref)

    @pl.when(j == 0)
    def _():
        st_ref[...] = jnp.zeros_like(st_ref)
        halo_ref[...] = jnp.zeros_like(halo_ref)

    wpo_ref, wgo_ref, wo_ref = wsq_ref.at[0], wsq_ref.at[1], wsq_ref.at[2]
    x = x_ref[...]
    mod = mod_ref[pl.ds(b, 1), :]
    shift, scale, gate = mod[:, 0:d], mod[:, d:2 * d], mod[:, 2 * d:3 * d]
    gain = gnorm_ref[...] * (1.0 + scale)
    h = x * lax.rsqrt(jnp.mean(x * x, axis=-1, keepdims=True) + EPS) * gain + shift
    hb_ref[...] = h.astype(BF16)

    def proj(c0, c1, rows=slice(None)):
        return _dot(hb_ref[rows, :], win_ref[:, c0:c1])

    def proj_seg(name):
        return proj(*_IN_COLS[name])

    half = ts // 2
    halves = [slice(0, half), slice(half, ts)]
    k0, g1 = _IN_COLS["k"][0], _IN_COLS["decay_gate"][1]
    ka = proj(k0, g1)
    k = ka[:, 0:GLA_KEY_DIM]
    a_low = ka[:, GLA_KEY_DIM:GLA_KEY_DIM + LANES]
    wup = jnp.concatenate([wup_ref[...].astype(BF16), jnp.zeros((LANES - GLA_GATE_RANK, GLA_KEY_DIM), BF16)], axis=0)
    log_a = _log_sigmoid(_dot(a_low.astype(BF16), wup) + balpha_ref[...]) * (1.0 / GLA_GATE_NORMALIZER)
    la_hi = log_a.astype(BF16)
    la_lo = (log_a - la_hi.astype(F32)).astype(BF16)
    q = proj_seg("q")
    log_q_scale = -0.5 * math.log(GLA_HEAD_K)
    v = proj_seg("v")
    rr = lax.broadcasted_iota(jnp.int32, (GLA_BLOCK, GLA_BLOCK), 0)
    cc = lax.broadcasted_iota(jnp.int32, (GLA_BLOCK, GLA_BLOCK), 1)
    causal = rr >= cc
    tri = jnp.where(causal, 1.0, 0.0).astype(BF16)
    tri2 = jnp.concatenate([tri, tri], axis=1)
    nblk = ts // GLA_BLOCK
    cbs = []
    for r in range(nblk):
        rows = slice(r * GLA_BLOCK, (r + 1) * GLA_BLOCK)
        cbs.append(_dot(tri2, jnp.concatenate([la_hi[rows, :], la_lo[rows, :]], axis=0)))
    pv = proj_seg("pool_value")

    fillers = ["pool_gate", "merge_pool", "gla_gate", "merge_gla"]
    filled = []
    for r in range(nblk):
        rows = slice(r * GLA_BLOCK, (r + 1) * GLA_BLOCK)
        cb = cbs[r]
        b_mid = cb[GLA_BLOCK // 2 - 1:GLA_BLOCK // 2, :]
        b_end = cb[GLA_BLOCK - 1:GLA_BLOCK, :]
        q_mid = q[rows, :] * jnp.exp(cb - (b_mid - log_q_scale))
        k_mid = k[rows, :] * jnp.exp(b_mid - cb)
        q_in = q_mid.astype(BF16)
        k_in = k_mid.astype(BF16)
        q_start = (q_mid * jnp.exp(b_mid)).astype(BF16)
        k_end = (k_mid * jnp.exp(b_end - b_mid)).astype(BF16)
        block_decay = jnp.exp(b_end)
        heads = range(GLA_HEADS)
        kcs = [slice(hh * GLA_HEAD_K, (hh + 1) * GLA_HEAD_K) for hh in heads]
        vcs = [slice(hh * GLA_HEAD_V, (hh + 1) * GLA_HEAD_V) for hh in heads]
        scores = [lax.dot_general(q_in[:, kcs[hh]], k_in[:, kcs[hh]], _NT, preferred_element_type=F32) for hh in heads]
        vbs = [v[rows, vcs[hh]] for hh in heads]
        updates = [_dot(vbs[hh].T.astype(BF16), k_end[:, kcs[hh]]) for hh in heads]
        for hh in heads:
            state_t = st_ref[hh]
            o = _dot(jnp.where(causal, scores[hh], 0.0).astype(BF16), vbs[hh].astype(BF16))
            o = o + lax.dot_general(q_start[:, kcs[hh]], state_t.astype(BF16), _NT, preferred_element_type=F32)
            o_scr[rows, vcs[hh]] = o
            st_ref[hh] = state_t * block_decay[:, kcs[hh]] + updates[hh]
        if r < len(fillers):
            filled.append(proj_seg(fillers[r]))
    for name in fillers[len(filled):]:
        filled.append(proj_seg(name))
    pg, mgp, gg, mgg = filled
    pool_gate = _silu(pg).astype(BF16)
    merge_pool_gate = _sigmoid(mgp).astype(BF16)
    gla_gate = _silu(gg).astype(BF16)
    gate_gla = _sigmoid(mgg).astype(BF16)

    head_t_plus_1 = lax.broadcasted_iota(jnp.int32, (POOL_HALO, 1), 0) + (j * ts + 1)
    mixed = []
    for g, w in enumerate(POOL_WINDOWS):
        cols = slice(g * POOL_GROUP_DIM, (g + 1) * POOL_GROUP_DIM)
        win = jnp.concatenate([halo_ref[:, cols], pv[:, cols]], axis=0)
        step = 1
        while step < min(w, SUBLANES):
            win = win + pltpu.roll(win, step, axis=0)
            step *= 2
        if w > SUBLANES:
            win = win[POOL_HALO:, :] + win[POOL_HALO - SUBLANES:-SUBLANES, :]
        else:
            win = win[POOL_HALO:, :]
        head_inv_cnt = 1.0 / jnp.minimum(head_t_plus_1, w).astype(F32)
        mean = jnp.concatenate([win[:POOL_HALO, :] * head_inv_cnt, win[POOL_HALO:, :] * (1.0 / w)], axis=0)
        pooled = mean - pv[:, cols]
        mixed.append(_dot(pooled.astype(BF16), wgrp_ref[g].astype(BF16)))
    mixed = jnp.concatenate(mixed, axis=1)
    halo_ref[...] = pv[ts - POOL_HALO:, :]
    y_pool = mixed * pscale_ref[...] * pool_gate
    merged = merge_pool_gate * _dot(y_pool.astype(BF16), wpo_ref[...])

    o = o_scr[...]
    normed = []
    for hh in range(GLA_HEADS):
        oh = o[:, hh * GLA_HEAD_V:(hh + 1) * GLA_HEAD_V]
        normed.append(oh * lax.rsqrt(jnp.mean(oh * oh, axis=-1, keepdims=True) + EPS) * ghead_ref[...])
    y_gla = (jnp.concatenate(normed, axis=1) * gla_gate).astype(BF16)
    gfin = gfin_ref[...]
    g_out = [_dot(y_gla[hs, :], wgo_ref[...]) for hs in halves]
    for hs, g_half in zip(halves, g_out):
        merged_half = merged[hs, :] + gate_gla[hs, :] * g_half
        y = x_ref[hs, :] + gate * _dot(merged_half.astype(BF16), wo_ref[...])
        out_ref[hs, :] = y * lax.rsqrt(jnp.mean(y * y, axis=-1, keepdims=True) + EPS) * gfin


def _resident(shape):
    return pl.BlockSpec(shape, lambda b, j: (0,) * len(shape), pipeline_mode=pl.Buffered(1))


@functools.partial(jax.jit, static_argnames=("row_tile",))
def _forward(x, c, g_norm, w_ada, b_ada, w_in, w_pool_group, pool_scale, w_alpha_up, b_alpha, g_gla_head,
             w_pool_out, w_gla_out, w_out, g_final, row_tile=ROW_TILE):
    assert g_norm.shape[0] == 1, "single-layer stack"
    bsz, seq, d = x.shape
    ts = row_tile
    assert d == D_MODEL and seq % ts == 0 and ts % (2 * GLA_BLOCK) == 0
    assert w_in.shape == (1, d, IN_WIDTH_REF) and w_ada.shape == (1, d, N_MOD * STAGE)

    row = lambda a: a.reshape(1, -1).astype(F32)
    vmem_operands = [
        c, row(b_ada), row(g_norm), w_pool_group[0], row(pool_scale), w_alpha_up[0], row(b_alpha),
        row(g_gla_head), row(g_final),
    ]
    hbm_operands = [w_in[0].T, w_ada[0], w_pool_out[0], w_gla_out[0], w_out[0]]
    in_specs = [pl.BlockSpec((None, ts, d), lambda b, j: (b, j, 0))]
    in_specs += [_resident(a.shape) for a in vmem_operands]
    in_specs += [pl.BlockSpec(memory_space=pl.ANY) for _ in hbm_operands]

    return pl.pallas_call(
        _layer_kernel,
        out_shape=jax.ShapeDtypeStruct((bsz, seq, d), x.dtype),
        grid=(bsz, seq // ts),
        in_specs=in_specs,
        out_specs=pl.BlockSpec((None, ts, d), lambda b, j: (b, j, 0)),
        scratch_shapes=[pltpu.VMEM((d, IN_WIDTH_PACKED), BF16),
                        pltpu.VMEM((N_SQUARE, d, d), BF16),
                        pltpu.VMEM((bsz, N_MOD * d), F32),
                        pltpu.VMEM((2, STAGE, STAGE), F32),
                        pltpu.SemaphoreType.DMA((2,)),
                        pltpu.VMEM((GLA_HEADS, GLA_HEAD_V, GLA_HEAD_K), F32),
                        pltpu.VMEM((POOL_HALO, d), F32),
                        pltpu.VMEM((ts, GLA_VAL_DIM), F32),
                        pltpu.VMEM((ts, d), BF16)],
        compiler_params=pltpu.CompilerParams(dimension_semantics=("arbitrary", "arbitrary"),
                                             vmem_limit_bytes=VMEM_LIMIT_BYTES),
        name="hybrid_pool_gla_layer",
    )(x, *vmem_operands, *hbm_operands)


def kernel(x, c, g_norm, w_ada, b_ada, w_in, w_pool_group, pool_scale, w_alpha_up, b_alpha, g_gla_head,
           w_pool_out, w_gla_out, w_out, g_final):
    return _forward(x, c, g_norm, w_ada, b_ada, w_in, w_pool_group, pool_scale, w_alpha_up, b_alpha,
                    g_gla_head, w_pool_out, w_gla_out, w_out, g_final)
```

```python
import functools
import math

import jax
import jax.numpy as jnp
from jax import lax
from jax.experimental import pallas as pl
from jax.experimental.pallas import tpu as pltpu

F32 = jnp.float32
BF16 = jnp.bfloat16

D_MODEL = 1024
EPS = 1e-6
POOL_WINDOWS = (2, 4, 8, 16)
POOL_GROUP_DIM = D_MODEL // len(POOL_WINDOWS)
POOL_HALO = 16
GLA_HEADS = 4
GLA_KEY_DIM = D_MODEL // 2
GLA_VAL_DIM = D_MODEL
GLA_HEAD_K = GLA_KEY_DIM // GLA_HEADS
GLA_HEAD_V = GLA_VAL_DIM // GLA_HEADS
GLA_GATE_RANK = 16
GLA_GATE_NORMALIZER = 16.0
GLA_BLOCK = 128
LANES = 128
SUBLANES = 8
assert max(POOL_WINDOWS) <= 2 * SUBLANES <= POOL_HALO
ROW_TILE = 512
STAGE = 1024
VMEM_LIMIT_BYTES = 60 * 1024 * 1024

_IN_SEGMENTS = (("pool_value", D_MODEL), ("pool_gate", D_MODEL), ("q", GLA_KEY_DIM), ("k", GLA_KEY_DIM),
                ("decay_gate", LANES), ("v", GLA_VAL_DIM), ("gla_gate", GLA_VAL_DIM), ("merge_pool", D_MODEL),
                ("merge_gla", D_MODEL))
_IN_COLS = {}
_start = 0
for _name, _width in _IN_SEGMENTS:
    _IN_COLS[_name] = (_start, _start + _width)
    _start += _width
IN_WIDTH_PACKED = _start
_REF_GATE_COL = 2 * D_MODEL + 2 * GLA_KEY_DIM + 2 * GLA_VAL_DIM
IN_WIDTH_REF = _REF_GATE_COL + GLA_GATE_RANK + 2 * D_MODEL
_IN_CHUNKS = ((0, _IN_COLS["pool_value"][0]), (D_MODEL, _IN_COLS["pool_gate"][0]), (2 * D_MODEL, _IN_COLS["q"][0]),
              (2 * D_MODEL + 2 * GLA_KEY_DIM, _IN_COLS["v"][0]),
              (2 * D_MODEL + 2 * GLA_KEY_DIM + GLA_VAL_DIM, _IN_COLS["gla_gate"][0]),
              (_REF_GATE_COL + GLA_GATE_RANK, _IN_COLS["merge_pool"][0]),
              (_REF_GATE_COL + GLA_GATE_RANK + D_MODEL, _IN_COLS["merge_gla"][0]))
assert _IN_COLS["k"][0] == _IN_COLS["q"][0] + GLA_KEY_DIM and 2 * GLA_KEY_DIM == STAGE == D_MODEL
_HALVED_IN_SEGMENTS = ("pool_gate", "gla_gate", "merge_pool", "merge_gla")
_HALVED_SQUARE = (0, 1)
N_SQUARE = 3
N_MOD = 3

_NT = (((1,), (1,)), ((), ()))


def _dot(a, b):
    return jnp.dot(a, b, preferred_element_type=F32)


def _silu(x):
    half = 0.5 * x
    return half * jnp.tanh(half) + half


def _silu_of_half(half):
    return half * jnp.tanh(half) + half


def _twice_sigmoid_of_half(half):
    return jnp.tanh(half) + 1.0


def _log_sigmoid(x):
    return jnp.minimum(x, 0.0) - jnp.log1p(jnp.exp(-jnp.abs(x)))


def _prepare_weights(c_ref, bada_ref, wint_hbm, wada_hbm, wsq_hbms, win_ref, wsq_ref, mod_ref, stage_ref, sem_ref):
    gate_chunk = len(_IN_CHUNKS)
    n_chunks = gate_chunk + 1 + N_SQUARE + N_MOD

    def copy(i):
        slot = i % 2
        if i < gate_chunk:
            src = wint_hbm.at[pl.ds(_IN_CHUNKS[i][0], STAGE), :]
            dst = stage_ref.at[slot]
        elif i == gate_chunk:
            src = wint_hbm.at[pl.ds(_REF_GATE_COL, LANES), :]
            dst = stage_ref.at[slot, pl.ds(0, LANES), :]
        elif i < gate_chunk + 1 + N_SQUARE:
            src = wsq_hbms[i - gate_chunk - 1]
            dst = stage_ref.at[slot]
        else:
            src = wada_hbm.at[:, pl.ds((i - gate_chunk - 1 - N_SQUARE) * STAGE, STAGE)]
            dst = stage_ref.at[slot]
        return pltpu.make_async_copy(src, dst, sem_ref.at[slot])

    silu_c = _silu(c_ref[...]).astype(BF16)
    copy(0).start()
    for i in range(n_chunks):
        if i + 1 < n_chunks:
            copy(i + 1).start()
        copy(i).wait()
        slot = i % 2
        if i < gate_chunk:
            dst0 = _IN_CHUNKS[i][1]
            halved = any(_IN_COLS[name][0] == dst0 for name in _HALVED_IN_SEGMENTS)
            tile = stage_ref[slot].T
            win_ref[:, dst0:dst0 + STAGE] = (tile * 0.5 if halved else tile).astype(BF16)
        elif i == gate_chunk:
            tile = stage_ref[slot, 0:LANES, :].T
            lane = lax.broadcasted_iota(jnp.int32, tile.shape, 1)
            g0, g1 = _IN_COLS["decay_gate"]
            win_ref[:, g0:g1] = jnp.where(lane < GLA_GATE_RANK, tile, 0.0).astype(BF16)
        elif i < gate_chunk + 1 + N_SQUARE:
            sq = i - gate_chunk - 1
            tile = stage_ref[slot]
            wsq_ref[sq] = (tile * 0.5 if sq in _HALVED_SQUARE else tile).astype(BF16)
        else:
            c0 = (i - gate_chunk - 1 - N_SQUARE) * STAGE
            mod_ref[:, c0:c0 + STAGE] = _dot(silu_c, stage_ref[slot].astype(BF16)) + bada_ref[:, c0:c0 + STAGE]


def _layer_kernel(x_ref, c_ref, bada_ref, gnorm_ref, wgrp_ref, pscale_ref, wup_ref, balpha_ref, ghead_ref, gfin_ref,
                  wint_hbm, wada_hbm, wpo_hbm, wgo_hbm, wo_hbm, out_ref,
                  win_ref, wsq_ref, mod_ref, stage_ref, sem_ref, st_ref, halo_ref, o_scr, hb_ref):
    b = pl.program_id(0)
    j = pl.program_id(1)
    ts = x_ref.shape[0]
    d = D_MODEL

    @pl.when(jnp.logical_and(b == 0, j == 0))
    def _():
        _prepare_weights(c_ref, bada_ref, wint_hbm, wada_hbm, (wpo_hbm, wgo_hbm, wo_hbm), win_ref, wsq_ref, mod_ref,
                         stage_ref, sem_ref)

    @pl.when(j == 0)
    def _():
        st_ref[...] = jnp.zeros_like(st_ref)
        halo_ref[...] = jnp.zeros_like(halo_ref)

    wpo_ref, wgo_ref, wo_ref = wsq_ref.at[0], wsq_ref.at[1], wsq_ref.at[2]
    x = x_ref[...]
    mod = mod_ref[pl.ds(b, 1), :]
    shift, scale, gate = mod[:, 0:d], mod[:, d:2 * d], mod[:, 2 * d:3 * d]
    gain = gnorm_ref[...] * (1.0 + scale)
    h = x * lax.rsqrt(jnp.mean(x * x, axis=-1, keepdims=True) + EPS) * gain + shift
    hb_ref[...] = h.astype(BF16)

    def proj(c0, c1, rows=slice(None)):
        return _dot(hb_ref[rows, :], win_ref[:, c0:c1])

    def proj_seg(name):
        return proj(*_IN_COLS[name])

    half = ts // 2
    halves = [slice(0, half), slice(half, ts)]
    k0, g1 = _IN_COLS["k"][0], _IN_COLS["decay_gate"][1]
    ka = proj(k0, g1)
    k = ka[:, 0:GLA_KEY_DIM]
    a_low = ka[:, GLA_KEY_DIM:GLA_KEY_DIM + LANES]
    wup = jnp.concatenate([wup_ref[...].astype(BF16), jnp.zeros((LANES - GLA_GATE_RANK, GLA_KEY_DIM), BF16)], axis=0)
    log_a = _log_sigmoid(_dot(a_low.astype(BF16), wup) + balpha_ref[...]) * (1.0 / GLA_GATE_NORMALIZER)
    la_hi = log_a.astype(BF16)
    la_lo = (log_a - la_hi.astype(F32)).astype(BF16)
    q = proj_seg("q")
    log_q_scale = -0.5 * math.log(GLA_HEAD_K)
    v = proj_seg("v")
    rr = lax.broadcasted_iota(jnp.int32, (GLA_BLOCK, GLA_BLOCK), 0)
    cc = lax.broadcasted_iota(jnp.int32, (GLA_BLOCK, GLA_BLOCK), 1)
    causal = rr >= cc
    tri = jnp.where(causal, 1.0, 0.0).astype(BF16)
    tri2 = jnp.concatenate([tri, tri], axis=1)
    nblk = ts // GLA_BLOCK
    cbs = []
    for r in range(nblk):
        rows = slice(r * GLA_BLOCK, (r + 1) * GLA_BLOCK)
        cbs.append(_dot(tri2, jnp.concatenate([la_hi[rows, :], la_lo[rows, :]], axis=0)))
    pv = proj_seg("pool_value")

    fillers = ["pool_gate", "merge_pool", "gla_gate", "merge_gla"]
    filled = []
    for r in range(nblk):
        rows = slice(r * GLA_BLOCK, (r + 1) * GLA_BLOCK)
        cb = cbs[r]
        b_mid = cb[GLA_BLOCK // 2 - 1:GLA_BLOCK // 2, :]
        b_end = cb[GLA_BLOCK - 1:GLA_BLOCK, :]
        q_mid = q[rows, :] * jnp.exp(cb - (b_mid - log_q_scale))
        k_mid = k[rows, :] * jnp.exp(b_mid - cb)
        q_in = q_mid.astype(BF16)
        k_in = k_mid.astype(BF16)
        q_start = (q_mid * jnp.exp(b_mid)).astype(BF16)
        k_end = (k_mid * jnp.exp(b_end - b_mid)).astype(BF16)
        block_decay = jnp.exp(b_end)
        heads = range(GLA_HEADS)
        kcs = [slice(hh * GLA_HEAD_K, (hh + 1) * GLA_HEAD_K) for hh in heads]
        vcs = [slice(hh * GLA_HEAD_V, (hh + 1) * GLA_HEAD_V) for hh in heads]
        scores = [lax.dot_general(q_in[:, kcs[hh]], k_in[:, kcs[hh]], _NT, preferred_element_type=F32) for hh in heads]
        vbs = [v[rows, vcs[hh]] for hh in heads]
        updates = [_dot(vbs[hh].T.astype(BF16), k_end[:, kcs[hh]]) for hh in heads]
        for hh in heads:
            state_t = st_ref[hh]
            o = _dot(jnp.where(causal, scores[hh], 0.0).astype(BF16), vbs[hh].astype(BF16))
            o = o + lax.dot_general(q_start[:, kcs[hh]], state_t.astype(BF16), _NT, preferred_element_type=F32)
            o_scr[rows, vcs[hh]] = o
            st_ref[hh] = state_t * block_decay[:, kcs[hh]] + updates[hh]
        if r < len(fillers):
            filled.append(proj_seg(fillers[r]))
    for name in fillers[len(filled):]:
        filled.append(proj_seg(name))
    pg, mgp, gg, mgg = filled
    pool_gate = _silu_of_half(pg).astype(BF16)
    merge_pool_gate = _twice_sigmoid_of_half(mgp).astype(BF16)
    gla_gate = _silu_of_half(gg).astype(BF16)
    gate_gla = _twice_sigmoid_of_half(mgg).astype(BF16)

    head_t_plus_1 = lax.broadcasted_iota(jnp.int32, (POOL_HALO, 1), 0) + (j * ts + 1)
    mixed = []
    for g, w in enumerate(POOL_WINDOWS):
        cols = slice(g * POOL_GROUP_DIM, (g + 1) * POOL_GROUP_DIM)
        win = jnp.concatenate([halo_ref[:, cols], pv[:, cols]], axis=0)
        step = 1
        while step < min(w, SUBLANES):
            win = win + pltpu.roll(win, step, axis=0)
            step *= 2
        if w > SUBLANES:
            win = win[POOL_HALO:, :] + win[POOL_HALO - SUBLANES:-SUBLANES, :]
        else:
            win = win[POOL_HALO:, :]
        head_inv_cnt = 1.0 / jnp.minimum(head_t_plus_1, w).astype(F32)
        mean = jnp.concatenate([win[:POOL_HALO, :] * head_inv_cnt, win[POOL_HALO:, :] * (1.0 / w)], axis=0)
        pooled = mean - pv[:, cols]
        mixed.append(_dot(pooled.astype(BF16), wgrp_ref[g].astype(BF16)))
    mixed = jnp.concatenate(mixed, axis=1)
    halo_ref[...] = pv[ts - POOL_HALO:, :]
    y_pool = mixed * pscale_ref[...] * pool_gate
    merged = merge_pool_gate * _dot(y_pool.astype(BF16), wpo_ref[...])

    o = o_scr[...]
    normed = []
    for hh in range(GLA_HEADS):
        oh = o[:, hh * GLA_HEAD_V:(hh + 1) * GLA_HEAD_V]
        normed.append(oh * lax.rsqrt(jnp.mean(oh * oh, axis=-1, keepdims=True) + EPS) * ghead_ref[...])
    y_gla = (jnp.concatenate(normed, axis=1) * gla_gate).astype(BF16)
    gfin = gfin_ref[...]
    g_out = [_dot(y_gla[hs, :], wgo_ref[...]) for hs in halves]
    for hs, g_half in zip(halves, g_out):
        merged_half = merged[hs, :] + gate_gla[hs, :] * g_half
        y = x_ref[hs, :] + gate * _dot(merged_half.astype(BF16), wo_ref[...])
        out_ref[hs, :] = y * lax.rsqrt(jnp.mean(y * y, axis=-1, keepdims=True) + EPS) * gfin


def _resident(shape):
    return pl.BlockSpec(shape, lambda b, j: (0,) * len(shape), pipeline_mode=pl.Buffered(1))


@functools.partial(jax.jit, static_argnames=("row_tile",))
def _forward(x, c, g_norm, w_ada, b_ada, w_in, w_pool_group, pool_scale, w_alpha_up, b_alpha, g_gla_head,
             w_pool_out, w_gla_out, w_out, g_final, row_tile=ROW_TILE):
    assert g_norm.shape[0] == 1, "single-layer stack"
    bsz, seq, d = x.shape
    ts = row_tile
    assert d == D_MODEL and seq % ts == 0 and ts % (2 * GLA_BLOCK) == 0
    assert w_in.shape == (1, d, IN_WIDTH_REF) and w_ada.shape == (1, d, N_MOD * STAGE)

    row = lambda a: a.reshape(1, -1).astype(F32)
    vmem_operands = [
        c, row(b_ada), row(g_norm), w_pool_group[0], row(pool_scale), w_alpha_up[0], row(b_alpha),
        row(g_gla_head), row(g_final),
    ]
    hbm_operands = [w_in[0].T, w_ada[0], w_pool_out[0], w_gla_out[0], w_out[0]]
    in_specs = [pl.BlockSpec((None, ts, d), lambda b, j: (b, j, 0))]
    in_specs += [_resident(a.shape) for a in vmem_operands]
    in_specs += [pl.BlockSpec(memory_space=pl.ANY) for _ in hbm_operands]

    return pl.pallas_call(
        _layer_kernel,
        out_shape=jax.ShapeDtypeStruct((bsz, seq, d), x.dtype),
        grid=(bsz, seq // ts),
        in_specs=in_specs,
        out_specs=pl.BlockSpec((None, ts, d), lambda b, j: (b, j, 0)),
        scratch_shapes=[pltpu.VMEM((d, IN_WIDTH_PACKED), BF16),
                        pltpu.VMEM((N_SQUARE, d, d), BF16),
                        pltpu.VMEM((bsz, N_MOD * d), F32),
                        pltpu.VMEM((2, STAGE, STAGE), F32),
                        pltpu.SemaphoreType.DMA((2,)),
                        pltpu.VMEM((GLA_HEADS, GLA_HEAD_V, GLA_HEAD_K), F32),
                        pltpu.VMEM((POOL_HALO, d), F32),
                        pltpu.VMEM((ts, GLA_VAL_DIM), F32),
                        pltpu.VMEM((ts, d), BF16)],
        compiler_params=pltpu.CompilerParams(dimension_semantics=("arbitrary", "arbitrary"),
                                             vmem_limit_bytes=VMEM_LIMIT_BYTES),
        name="hybrid_pool_gla_layer",
    )(x, *vmem_operands, *hbm_operands)


def kernel(x, c, g_norm, w_ada, b_ada, w_in, w_pool_group, pool_scale, w_alpha_up, b_alpha, g_gla_head,
           w_pool_out, w_gla_out, w_out, g_final):
    return _forward(x, c, g_norm, w_ada, b_ada, w_in, w_pool_group, pool_scale, w_alpha_up, b_alpha,
                    g_gla_head, w_pool_out, w_gla_out, w_out, g_final)
```

```python
import functools
import math

import jax
import jax.numpy as jnp
from jax import lax
from jax.experimental import pallas as pl
from jax.experimental.pallas import tpu as pltpu

F32 = jnp.float32
BF16 = jnp.bfloat16

D_MODEL = 1024
EPS = 1e-6
POOL_WINDOWS = (2, 4, 8, 16)
POOL_GROUP_DIM = D_MODEL // len(POOL_WINDOWS)
POOL_HALO = 16
GLA_HEADS = 4
GLA_KEY_DIM = D_MODEL // 2
GLA_VAL_DIM = D_MODEL
GLA_HEAD_K = GLA_KEY_DIM // GLA_HEADS
GLA_HEAD_V = GLA_VAL_DIM // GLA_HEADS
GLA_GATE_RANK = 16
GLA_GATE_NORMALIZER = 16.0
LOG2_E = 1.4426950408889634
GLA_BLOCK = 128
LANES = 128
SUBLANES = 8
assert max(POOL_WINDOWS) <= 2 * SUBLANES <= POOL_HALO
ROW_TILE = 512
STAGE = 1024
VMEM_LIMIT_BYTES = 60 * 1024 * 1024

_IN_SEGMENTS = (("pool_value", D_MODEL), ("pool_gate", D_MODEL), ("q", GLA_KEY_DIM), ("k", GLA_KEY_DIM),
                ("decay_gate", LANES), ("v", GLA_VAL_DIM), ("gla_gate", GLA_VAL_DIM), ("merge_pool", D_MODEL),
                ("merge_gla", D_MODEL))
_IN_COLS = {}
_start = 0
for _name, _width in _IN_SEGMENTS:
    _IN_COLS[_name] = (_start, _start + _width)
    _start += _width
IN_WIDTH_PACKED = _start
_REF_GATE_COL = 2 * D_MODEL + 2 * GLA_KEY_DIM + 2 * GLA_VAL_DIM
IN_WIDTH_REF = _REF_GATE_COL + GLA_GATE_RANK + 2 * D_MODEL
_IN_CHUNKS = ((0, _IN_COLS["pool_value"][0]), (D_MODEL, _IN_COLS["pool_gate"][0]), (2 * D_MODEL, _IN_COLS["q"][0]),
              (2 * D_MODEL + 2 * GLA_KEY_DIM, _IN_COLS["v"][0]),
              (2 * D_MODEL + 2 * GLA_KEY_DIM + GLA_VAL_DIM, _IN_COLS["gla_gate"][0]),
              (_REF_GATE_COL + GLA_GATE_RANK, _IN_COLS["merge_pool"][0]),
              (_REF_GATE_COL + GLA_GATE_RANK + D_MODEL, _IN_COLS["merge_gla"][0]))
assert _IN_COLS["k"][0] == _IN_COLS["q"][0] + GLA_KEY_DIM and 2 * GLA_KEY_DIM == STAGE == D_MODEL
N_SQUARE = 3
N_MOD = 3

_NT = (((1,), (1,)), ((), ()))


def _dot(a, b):
    return jnp.dot(a, b, preferred_element_type=F32)


def _sigmoid(x):
    return 0.5 * jnp.tanh(0.5 * x) + 0.5


def _silu(x):
    half = 0.5 * x
    return half * jnp.tanh(half) + half


def _log_sigmoid(x):
    return jnp.minimum(x, 0.0) - jnp.log1p(jnp.exp(-jnp.abs(x)))


def _prepare_weights(c_ref, bada_ref, wint_hbm, wada_hbm, wsq_hbms, win_ref, wsq_ref, mod_ref, stage_ref, sem_ref):
    gate_chunk = len(_IN_CHUNKS)
    n_chunks = gate_chunk + 1 + N_SQUARE + N_MOD

    def copy(i):
        slot = i % 2
        if i < gate_chunk:
            src = wint_hbm.at[pl.ds(_IN_CHUNKS[i][0], STAGE), :]
            dst = stage_ref.at[slot]
        elif i == gate_chunk:
            src = wint_hbm.at[pl.ds(_REF_GATE_COL, LANES), :]
            dst = stage_ref.at[slot, pl.ds(0, LANES), :]
        elif i < gate_chunk + 1 + N_SQUARE:
            src = wsq_hbms[i - gate_chunk - 1]
            dst = stage_ref.at[slot]
        else:
            src = wada_hbm.at[:, pl.ds((i - gate_chunk - 1 - N_SQUARE) * STAGE, STAGE)]
            dst = stage_ref.at[slot]
        return pltpu.make_async_copy(src, dst, sem_ref.at[slot])

    silu_c = _silu(c_ref[...]).astype(BF16)
    copy(0).start()
    for i in range(n_chunks):
        if i + 1 < n_chunks:
            copy(i + 1).start()
        copy(i).wait()
        slot = i % 2
        if i < gate_chunk:
            dst0 = _IN_CHUNKS[i][1]
            win_ref[:, dst0:dst0 + STAGE] = stage_ref[slot].T.astype(BF16)
        elif i == gate_chunk:
            tile = stage_ref[slot, 0:LANES, :].T
            lane = lax.broadcasted_iota(jnp.int32, tile.shape, 1)
            g0, g1 = _IN_COLS["decay_gate"]
            win_ref[:, g0:g1] = jnp.where(lane < GLA_GATE_RANK, tile, 0.0).astype(BF16)
        elif i < gate_chunk + 1 + N_SQUARE:
            wsq_ref[i - gate_chunk - 1] = stage_ref[slot].astype(BF16)
        else:
            c0 = (i - gate_chunk - 1 - N_SQUARE) * STAGE
            mod_ref[:, c0:c0 + STAGE] = _dot(silu_c, stage_ref[slot].astype(BF16)) + bada_ref[:, c0:c0 + STAGE]


def _layer_kernel(x_ref, c_ref, bada_ref, gnorm_ref, wgrp_ref, pscale_ref, wup_ref, balpha_ref, ghead_ref, gfin_ref,
                  wint_hbm, wada_hbm, wpo_hbm, wgo_hbm, wo_hbm, out_ref,
                  win_ref, wsq_ref, mod_ref, stage_ref, sem_ref, st_ref, halo_ref, o_scr, hb_ref):
    b = pl.program_id(0)
    j = pl.program_id(1)
    ts = x_ref.shape[0]
    d = D_MODEL

    @pl.when(jnp.logical_and(b == 0, j == 0))
    def _():
        _prepare_weights(c_ref, bada_ref, wint_hbm, wada_hbm, (wpo_hbm, wgo_hbm, wo_hbm), win_ref, wsq_ref, mod_ref,
                         stage_ref, sem_ref)

    @pl.when(j == 0)
    def _():
        st_ref[...] = jnp.zeros_like(st_ref)
        halo_ref[...] = jnp.zeros_like(halo_ref)

    wpo_ref, wgo_ref, wo_ref = wsq_ref.at[0], wsq_ref.at[1], wsq_ref.at[2]
    x = x_ref[...]
    mod = mod_ref[pl.ds(b, 1), :]
    shift, scale, gate = mod[:, 0:d], mod[:, d:2 * d], mod[:, 2 * d:3 * d]
    gain = gnorm_ref[...] * (1.0 + scale)
    h = x * lax.rsqrt(jnp.mean(x * x, axis=-1, keepdims=True) + EPS) * gain + shift
    hb_ref[...] = h.astype(BF16)

    def proj(c0, c1, rows=slice(None)):
        return _dot(hb_ref[rows, :], win_ref[:, c0:c1])

    def proj_seg(name):
        return proj(*_IN_COLS[name])

    half = ts // 2
    halves = [slice(0, half), slice(half, ts)]
    k0, g1 = _IN_COLS["k"][0], _IN_COLS["decay_gate"][1]
    ka = proj(k0, g1)
    k = ka[:, 0:GLA_KEY_DIM]
    a_low = ka[:, GLA_KEY_DIM:GLA_KEY_DIM + LANES]
    wup = jnp.concatenate([wup_ref[...].astype(BF16), jnp.zeros((LANES - GLA_GATE_RANK, GLA_KEY_DIM), BF16)], axis=0)
    log_a = _log_sigmoid(_dot(a_low.astype(BF16), wup) + balpha_ref[...]) * (LOG2_E / GLA_GATE_NORMALIZER)
    la_hi = log_a.astype(BF16)
    la_lo = (log_a - la_hi.astype(F32)).astype(BF16)
    q = proj_seg("q")
    log_q_scale = -0.5 * math.log2(GLA_HEAD_K)
    v = proj_seg("v")
    rr = lax.broadcasted_iota(jnp.int32, (GLA_BLOCK, GLA_BLOCK), 0)
    cc = lax.broadcasted_iota(jnp.int32, (GLA_BLOCK, GLA_BLOCK), 1)
    causal = rr >= cc
    tri = jnp.where(causal, 1.0, 0.0).astype(BF16)
    tri2 = jnp.concatenate([tri, tri], axis=1)
    nblk = ts // GLA_BLOCK
    cbs = []
    for r in range(nblk):
        rows = slice(r * GLA_BLOCK, (r + 1) * GLA_BLOCK)
        cbs.append(_dot(tri2, jnp.concatenate([la_hi[rows, :], la_lo[rows, :]], axis=0)))
    pv = proj_seg("pool_value")

    fillers = ["pool_gate", "merge_pool", "gla_gate", "merge_gla"]
    filled = []
    for r in range(nblk):
        rows = slice(r * GLA_BLOCK, (r + 1) * GLA_BLOCK)
        cb = cbs[r]
        b_mid = cb[GLA_BLOCK // 2 - 1:GLA_BLOCK // 2, :]
        b_end = cb[GLA_BLOCK - 1:GLA_BLOCK, :]
        q_mid = q[rows, :] * jnp.exp2(cb - (b_mid - log_q_scale))
        k_mid = k[rows, :] * jnp.exp2(b_mid - cb)
        q_in = q_mid.astype(BF16)
        k_in = k_mid.astype(BF16)
        q_start = (q_mid * jnp.exp2(b_mid)).astype(BF16)
        k_end = (k_mid * jnp.exp2(b_end - b_mid)).astype(BF16)
        block_decay = jnp.exp2(b_end)
        heads = range(GLA_HEADS)
        kcs = [slice(hh * GLA_HEAD_K, (hh + 1) * GLA_HEAD_K) for hh in heads]
        vcs = [slice(hh * GLA_HEAD_V, (hh + 1) * GLA_HEAD_V) for hh in heads]
        scores = [lax.dot_general(q_in[:, kcs[hh]], k_in[:, kcs[hh]], _NT, preferred_element_type=F32) for hh in heads]
        vbs = [v[rows, vcs[hh]] for hh in heads]
        updates = [_dot(vbs[hh].T.astype(BF16), k_end[:, kcs[hh]]) for hh in heads]
        for hh in heads:
            state_t = st_ref[hh]
            o = _dot(jnp.where(causal, scores[hh], 0.0).astype(BF16), vbs[hh].astype(BF16))
            o = o + lax.dot_general(q_start[:, kcs[hh]], state_t.astype(BF16), _NT, preferred_element_type=F32)
            o_scr[rows, vcs[hh]] = o
            st_ref[hh] = state_t * block_decay[:, kcs[hh]] + updates[hh]
        if r < len(fillers):
            filled.append(proj_seg(fillers[r]))
    for name in fillers[len(filled):]:
        filled.append(proj_seg(name))
    pg, mgp, gg, mgg = filled
    pool_gate = _silu(pg).astype(BF16)
    merge_pool_gate = _sigmoid(mgp).astype(BF16)
    gla_gate = _silu(gg).astype(BF16)
    gate_gla = _sigmoid(mgg).astype(BF16)

    head_t_plus_1 = lax.broadcasted_iota(jnp.int32, (POOL_HALO, 1), 0) + (j * ts + 1)
    mixed = []
    for g, w in enumerate(POOL_WINDOWS):
        cols = slice(g * POOL_GROUP_DIM, (g + 1) * POOL_GROUP_DIM)
        win = jnp.concatenate([halo_ref[:, cols], pv[:, cols]], axis=0)
        step = 1
        while step < min(w, SUBLANES):
            win = win + pltpu.roll(win, step, axis=0)
            step *= 2
        if w > SUBLANES:
            win = win[POOL_HALO:, :] + win[POOL_HALO - SUBLANES:-SUBLANES, :]
        else:
            win = win[POOL_HALO:, :]
        head_inv_cnt = 1.0 / jnp.minimum(head_t_plus_1, w).astype(F32)
        mean = jnp.concatenate([win[:POOL_HALO, :] * head_inv_cnt, win[POOL_HALO:, :] * (1.0 / w)], axis=0)
        pooled = mean - pv[:, cols]
        mixed.append(_dot(pooled.astype(BF16), wgrp_ref[g].astype(BF16)))
    mixed = jnp.concatenate(mixed, axis=1)
    halo_ref[...] = pv[ts - POOL_HALO:, :]
    y_pool = mixed * pscale_ref[...] * pool_gate
    merged = merge_pool_gate * _dot(y_pool.astype(BF16), wpo_ref[...])

    o = o_scr[...]
    normed = []
    for hh in range(GLA_HEADS):
        oh = o[:, hh * GLA_HEAD_V:(hh + 1) * GLA_HEAD_V]
        normed.append(oh * lax.rsqrt(jnp.mean(oh * oh, axis=-1, keepdims=True) + EPS) * ghead_ref[...])
    y_gla = (jnp.concatenate(normed, axis=1) * gla_gate).astype(BF16)
    gfin = gfin_ref[...]
    g_out = [_dot(y_gla[hs, :], wgo_ref[...]) for hs in halves]
    for hs, g_half in zip(halves, g_out):
        merged_half = merged[hs, :] + gate_gla[hs, :] * g_half
        y = x_ref[hs, :] + gate * _dot(merged_half.astype(BF16), wo_ref[...])
        out_ref[hs, :] = y * lax.rsqrt(jnp.mean(y * y, axis=-1, keepdims=True) + EPS) * gfin


def _resident(shape):
    return pl.BlockSpec(shape, lambda b, j: (0,) * len(shape), pipeline_mode=pl.Buffered(1))


@functools.partial(jax.jit, static_argnames=("row_tile",))
def _forward(x, c, g_norm, w_ada, b_ada, w_in, w_pool_group, pool_scale, w_alpha_up, b_alpha, g_gla_head,
             w_pool_out, w_gla_out, w_out, g_final, row_tile=ROW_TILE):
    assert g_norm.shape[0] == 1, "single-layer stack"
    bsz, seq, d = x.shape
    ts = row_tile
    assert d == D_MODEL and seq % ts == 0 and ts % (2 * GLA_BLOCK) == 0
    assert w_in.shape == (1, d, IN_WIDTH_REF) and w_ada.shape == (1, d, N_MOD * STAGE)

    row = lambda a: a.reshape(1, -1).astype(F32)
    vmem_operands = [
        c, row(b_ada), row(g_norm), w_pool_group[0], row(pool_scale), w_alpha_up[0], row(b_alpha),
        row(g_gla_head), row(g_final),
    ]
    hbm_operands = [w_in[0].T, w_ada[0], w_pool_out[0], w_gla_out[0], w_out[0]]
    in_specs = [pl.BlockSpec((None, ts, d), lambda b, j: (b, j, 0))]
    in_specs += [_resident(a.shape) for a in vmem_operands]
    in_specs += [pl.BlockSpec(memory_space=pl.ANY) for _ in hbm_operands]

    return pl.pallas_call(
        _layer_kernel,
        out_shape=jax.ShapeDtypeStruct((bsz, seq, d), x.dtype),
        grid=(bsz, seq // ts),
        in_specs=in_specs,
        out_specs=pl.BlockSpec((None, ts, d), lambda b, j: (b, j, 0)),
        scratch_shapes=[pltpu.VMEM((d, IN_WIDTH_PACKED), BF16),
                        pltpu.VMEM((N_SQUARE, d, d), BF16),
                        pltpu.VMEM((bsz, N_MOD * d), F32),
                        pltpu.VMEM((2, STAGE, STAGE), F32),
                        pltpu.SemaphoreType.DMA((2,)),
                        pltpu.VMEM((GLA_HEADS, GLA_HEAD_V, GLA_HEAD_K), F32),
                        pltpu.VMEM((POOL_HALO, d), F32),
                        pltpu.VMEM((ts, GLA_VAL_DIM), F32),
                        pltpu.VMEM((ts, d), BF16)],
        compiler_params=pltpu.CompilerParams(dimension_semantics=("arbitrary", "arbitrary"),
                                             vmem_limit_bytes=VMEM_LIMIT_BYTES),
        name="hybrid_pool_gla_layer",
    )(x, *vmem_operands, *hbm_operands)


def kernel(x, c, g_norm, w_ada, b_ada, w_in, w_pool_group, pool_scale, w_alpha_up, b_alpha, g_gla_head,
           w_pool_out, w_gla_out, w_out, g_final):
    return _forward(x, c, g_norm, w_ada, b_ada, w_in, w_pool_group, pool_scale, w_alpha_up, b_alpha,
                    g_gla_head, w_pool_out, w_gla_out, w_out, g_final)
```

```python
import functools
import math

import jax
import jax.numpy as jnp
from jax import lax
from jax.experimental import pallas as pl
from jax.experimental.pallas import tpu as pltpu

F32 = jnp.float32
BF16 = jnp.bfloat16

D_MODEL = 1024
EPS = 1e-6
POOL_WINDOWS = (2, 4, 8, 16)
POOL_GROUP_DIM = D_MODEL // len(POOL_WINDOWS)
POOL_HALO = 16
GLA_HEADS = 4
GLA_KEY_DIM = D_MODEL // 2
GLA_VAL_DIM = D_MODEL
GLA_HEAD_K = GLA_KEY_DIM // GLA_HEADS
GLA_HEAD_V = GLA_VAL_DIM // GLA_HEADS
GLA_GATE_RANK = 16
GLA_GATE_NORMALIZER = 16.0
GLA_BLOCK = 128
LANES = 128
SUBLANES = 8
assert max(POOL_WINDOWS) <= 2 * SUBLANES <= POOL_HALO
ROW_TILE = 512
STAGE = 1024
VMEM_LIMIT_BYTES = 60 * 1024 * 1024

_IN_SEGMENTS = (("pool_value", D_MODEL), ("pool_gate", D_MODEL), ("q", GLA_KEY_DIM), ("k", GLA_KEY_DIM),
                ("decay_gate", LANES), ("v", GLA_VAL_DIM), ("gla_gate", GLA_VAL_DIM), ("merge_pool", D_MODEL),
                ("merge_gla", D_MODEL))
_IN_COLS = {}
_start = 0
for _name, _width in _IN_SEGMENTS:
    _IN_COLS[_name] = (_start, _start + _width)
    _start += _width
IN_WIDTH_PACKED = _start
_REF_GATE_COL = 2 * D_MODEL + 2 * GLA_KEY_DIM + 2 * GLA_VAL_DIM
IN_WIDTH_REF = _REF_GATE_COL + GLA_GATE_RANK + 2 * D_MODEL
_IN_CHUNKS = ((0, _IN_COLS["pool_value"][0]), (D_MODEL, _IN_COLS["pool_gate"][0]), (2 * D_MODEL, _IN_COLS["q"][0]),
              (2 * D_MODEL + 2 * GLA_KEY_DIM, _IN_COLS["v"][0]),
              (2 * D_MODEL + 2 * GLA_KEY_DIM + GLA_VAL_DIM, _IN_COLS["gla_gate"][0]),
              (_REF_GATE_COL + GLA_GATE_RANK, _IN_COLS["merge_pool"][0]),
              (_REF_GATE_COL + GLA_GATE_RANK + D_MODEL, _IN_COLS["merge_gla"][0]))
assert _IN_COLS["k"][0] == _IN_COLS["q"][0] + GLA_KEY_DIM and 2 * GLA_KEY_DIM == STAGE == D_MODEL
N_SQUARE = 3
N_MOD = 3

_NT = (((1,), (1,)), ((), ()))


def _dot(a, b):
    return jnp.dot(a, b, preferred_element_type=F32)


def _sigmoid(x):
    return 0.5 * jnp.tanh(0.5 * x) + 0.5


def _silu(x):
    half = 0.5 * x
    return half * jnp.tanh(half) + half


def _log_sigmoid(x):
    return jnp.minimum(x, 0.0) - jnp.log(1.0 + jnp.exp(-jnp.abs(x)))


def _prepare_weights(c_ref, bada_ref, wint_hbm, wada_hbm, wsq_hbms, win_ref, wsq_ref, mod_ref, stage_ref, sem_ref):
    gate_chunk = len(_IN_CHUNKS)
    n_chunks = gate_chunk + 1 + N_SQUARE + N_MOD

    def copy(i):
        slot = i % 2
        if i < gate_chunk:
            src = wint_hbm.at[pl.ds(_IN_CHUNKS[i][0], STAGE), :]
            dst = stage_ref.at[slot]
        elif i == gate_chunk:
            src = wint_hbm.at[pl.ds(_REF_GATE_COL, LANES), :]
            dst = stage_ref.at[slot, pl.ds(0, LANES), :]
        elif i < gate_chunk + 1 + N_SQUARE:
            src = wsq_hbms[i - gate_chunk - 1]
            dst = stage_ref.at[slot]
        else:
            src = wada_hbm.at[:, pl.ds((i - gate_chunk - 1 - N_SQUARE) * STAGE, STAGE)]
            dst = stage_ref.at[slot]
        return pltpu.make_async_copy(src, dst, sem_ref.at[slot])

    silu_c = _silu(c_ref[...]).astype(BF16)
    copy(0).start()
    for i in range(n_chunks):
        if i + 1 < n_chunks:
            copy(i + 1).start()
        copy(i).wait()
        slot = i % 2
        if i < gate_chunk:
            dst0 = _IN_CHUNKS[i][1]
            win_ref[:, dst0:dst0 + STAGE] = stage_ref[slot].T.astype(BF16)
        elif i == gate_chunk:
            tile = stage_ref[slot, 0:LANES, :].T
            lane = lax.broadcasted_iota(jnp.int32, tile.shape, 1)
            g0, g1 = _IN_COLS["decay_gate"]
            win_ref[:, g0:g1] = jnp.where(lane < GLA_GATE_RANK, tile, 0.0).astype(BF16)
        elif i < gate_chunk + 1 + N_SQUARE:
            wsq_ref[i - gate_chunk - 1] = stage_ref[slot].astype(BF16)
        else:
            c0 = (i - gate_chunk - 1 - N_SQUARE) * STAGE
            mod_ref[:, c0:c0 + STAGE] = _dot(silu_c, stage_ref[slot].astype(BF16)) + bada_ref[:, c0:c0 + STAGE]


def _layer_kernel(x_ref, c_ref, bada_ref, gnorm_ref, wgrp_ref, pscale_ref, wup_ref, balpha_ref, ghead_ref, gfin_ref,
                  wint_hbm, wada_hbm, wpo_hbm, wgo_hbm, wo_hbm, out_ref,
                  win_ref, wsq_ref, mod_ref, stage_ref, sem_ref, st_ref, halo_ref, y_gla_ref, hb_ref):
    b = pl.program_id(0)
    j = pl.program_id(1)
    ts = x_ref.shape[0]
    d = D_MODEL

    @pl.when(jnp.logical_and(b == 0, j == 0))
    def _():
        _prepare_weights(c_ref, bada_ref, wint_hbm, wada_hbm, (wpo_hbm, wgo_hbm, wo_hbm), win_ref, wsq_ref, mod_ref,
                         stage_ref, sem_ref)

    @pl.when(j == 0)
    def _():
        st_ref[...] = jnp.zeros_like(st_ref)
        halo_ref[...] = jnp.zeros_like(halo_ref)

    wpo_ref, wgo_ref, wo_ref = wsq_ref.at[0], wsq_ref.at[1], wsq_ref.at[2]
    x = x_ref[...]
    mod = mod_ref[pl.ds(b, 1), :]
    shift, scale, gate = mod[:, 0:d], mod[:, d:2 * d], mod[:, 2 * d:3 * d]
    gain = gnorm_ref[...] * (1.0 + scale)
    h = x * lax.rsqrt(jnp.mean(x * x, axis=-1, keepdims=True) + EPS) * gain + shift
    hb_ref[...] = h.astype(BF16)

    def proj(c0, c1, rows=slice(None)):
        return _dot(hb_ref[rows, :], win_ref[:, c0:c1])

    def proj_seg(name):
        return proj(*_IN_COLS[name])

    half = ts // 2
    halves = [slice(0, half), slice(half, ts)]
    k0, g1 = _IN_COLS["k"][0], _IN_COLS["decay_gate"][1]
    ka = proj(k0, g1)
    k = ka[:, 0:GLA_KEY_DIM]
    a_low = ka[:, GLA_KEY_DIM:GLA_KEY_DIM + LANES]
    wup = jnp.concatenate([wup_ref[...].astype(BF16), jnp.zeros((LANES - GLA_GATE_RANK, GLA_KEY_DIM), BF16)], axis=0)
    log_a = _log_sigmoid(_dot(a_low.astype(BF16), wup) + balpha_ref[...]) * (1.0 / GLA_GATE_NORMALIZER)
    la_hi = log_a.astype(BF16)
    la_lo = (log_a - la_hi.astype(F32)).astype(BF16)
    q = proj_seg("q")
    log_q_scale = -0.5 * math.log(GLA_HEAD_K)
    v = proj_seg("v")
    rr = lax.broadcasted_iota(jnp.int32, (GLA_BLOCK, GLA_BLOCK), 0)
    cc = lax.broadcasted_iota(jnp.int32, (GLA_BLOCK, GLA_BLOCK), 1)
    causal = rr >= cc
    tri = jnp.where(causal, 1.0, 0.0).astype(BF16)
    tri2 = jnp.concatenate([tri, tri], axis=1)
    nblk = ts // GLA_BLOCK
    cbs = []
    for r in range(nblk):
        rows = slice(r * GLA_BLOCK, (r + 1) * GLA_BLOCK)
        cbs.append(_dot(tri2, jnp.concatenate([la_hi[rows, :], la_lo[rows, :]], axis=0)))
    pv = proj_seg("pool_value")
    gla_gate = _silu(proj_seg("gla_gate")).astype(BF16)
    ghead = ghead_ref[...]

    fillers = ["pool_gate", "merge_pool", "merge_gla"]
    filled = []
    for r in range(nblk):
        rows = slice(r * GLA_BLOCK, (r + 1) * GLA_BLOCK)
        cb = cbs[r]
        b_mid = cb[GLA_BLOCK // 2 - 1:GLA_BLOCK // 2, :]
        b_end = cb[GLA_BLOCK - 1:GLA_BLOCK, :]
        q_mid = q[rows, :] * jnp.exp(cb - (b_mid - log_q_scale))
        k_mid = k[rows, :] * jnp.exp(b_mid - cb)
        q_in = q_mid.astype(BF16)
        k_in = k_mid.astype(BF16)
        q_start = (q_mid * jnp.exp(b_mid)).astype(BF16)
        k_end = (k_mid * jnp.exp(b_end - b_mid)).astype(BF16)
        block_decay = jnp.exp(b_end)
        heads = range(GLA_HEADS)
        kcs = [slice(hh * GLA_HEAD_K, (hh + 1) * GLA_HEAD_K) for hh in heads]
        vcs = [slice(hh * GLA_HEAD_V, (hh + 1) * GLA_HEAD_V) for hh in heads]
        scores = [lax.dot_general(q_in[:, kcs[hh]], k_in[:, kcs[hh]], _NT, preferred_element_type=F32) for hh in heads]
        vbs = [v[rows, vcs[hh]] for hh in heads]
        updates = [_dot(vbs[hh].T.astype(BF16), k_end[:, kcs[hh]]) for hh in heads]
        for hh in heads:
            state_t = st_ref[hh]
            o = _dot(jnp.where(causal, scores[hh], 0.0).astype(BF16), vbs[hh].astype(BF16))
            o = o + lax.dot_general(q_start[:, kcs[hh]], state_t.astype(BF16), _NT, preferred_element_type=F32)
            normed = o * lax.rsqrt(jnp.mean(o * o, axis=-1, keepdims=True) + EPS) * ghead
            y_gla_ref[rows, vcs[hh]] = (normed * gla_gate[rows, vcs[hh]]).astype(BF16)
            st_ref[hh] = state_t * block_decay[:, kcs[hh]] + updates[hh]
        if r < len(fillers):
            filled.append(proj_seg(fillers[r]))
    for name in fillers[len(filled):]:
        filled.append(proj_seg(name))
    pg, mgp, mgg = filled
    pool_gate = _silu(pg).astype(BF16)
    merge_pool_gate = _sigmoid(mgp).astype(BF16)
    gate_gla = _sigmoid(mgg).astype(BF16)

    head_t_plus_1 = lax.broadcasted_iota(jnp.int32, (POOL_HALO, 1), 0) + (j * ts + 1)
    mixed = []
    for g, w in enumerate(POOL_WINDOWS):
        cols = slice(g * POOL_GROUP_DIM, (g + 1) * POOL_GROUP_DIM)
        win = jnp.concatenate([halo_ref[:, cols], pv[:, cols]], axis=0)
        step = 1
        while step < min(w, SUBLANES):
            win = win + pltpu.roll(win, step, axis=0)
            step *= 2
        if w > SUBLANES:
            win = win[POOL_HALO:, :] + win[POOL_HALO - SUBLANES:-SUBLANES, :]
        else:
            win = win[POOL_HALO:, :]
        head_inv_cnt = 1.0 / jnp.minimum(head_t_plus_1, w).astype(F32)
        mean = jnp.concatenate([win[:POOL_HALO, :] * head_inv_cnt, win[POOL_HALO:, :] * (1.0 / w)], axis=0)
        pooled = mean - pv[:, cols]
        mixed.append(_dot(pooled.astype(BF16), wgrp_ref[g].astype(BF16)))
    mixed = jnp.concatenate(mixed, axis=1)
    halo_ref[...] = pv[ts - POOL_HALO:, :]
    y_pool = mixed * pscale_ref[...] * pool_gate
    merged = merge_pool_gate * _dot(y_pool.astype(BF16), wpo_ref[...])

    gfin = gfin_ref[...]
    merged = merged + gate_gla * _dot(y_gla_ref[...], wgo_ref[...])
    for hs in halves:
        merged_half = merged[hs, :]
        y = x_ref[hs, :] + gate * _dot(merged_half.astype(BF16), wo_ref[...])
        out_ref[hs, :] = y * lax.rsqrt(jnp.mean(y * y, axis=-1, keepdims=True) + EPS) * gfin


def _resident(shape):
    return pl.BlockSpec(shape, lambda b, j: (0,) * len(shape), pipeline_mode=pl.Buffered(1))


@functools.partial(jax.jit, static_argnames=("row_tile",))
def _forward(x, c, g_norm, w_ada, b_ada, w_in, w_pool_group, pool_scale, w_alpha_up, b_alpha, g_gla_head,
             w_pool_out, w_gla_out, w_out, g_final, row_tile=ROW_TILE):
    assert g_norm.shape[0] == 1, "single-layer stack"
    bsz, seq, d = x.shape
    ts = row_tile
    assert d == D_MODEL and seq % ts == 0 and ts % (2 * GLA_BLOCK) == 0
    assert w_in.shape == (1, d, IN_WIDTH_REF) and w_ada.shape == (1, d, N_MOD * STAGE)

    row = lambda a: a.reshape(1, -1).astype(F32)
    vmem_operands = [
        c, row(b_ada), row(g_norm), w_pool_group[0], row(pool_scale), w_alpha_up[0], row(b_alpha),
        row(g_gla_head), row(g_final),
    ]
    hbm_operands = [w_in[0].T, w_ada[0], w_pool_out[0], w_gla_out[0], w_out[0]]
    in_specs = [pl.BlockSpec((None, ts, d), lambda b, j: (b, j, 0))]
    in_specs += [_resident(a.shape) for a in vmem_operands]
    in_specs += [pl.BlockSpec(memory_space=pl.ANY) for _ in hbm_operands]

    return pl.pallas_call(
        _layer_kernel,
        out_shape=jax.ShapeDtypeStruct((bsz, seq, d), x.dtype),
        grid=(bsz, seq // ts),
        in_specs=in_specs,
        out_specs=pl.BlockSpec((None, ts, d), lambda b, j: (b, j, 0)),
        scratch_shapes=[pltpu.VMEM((d, IN_WIDTH_PACKED), BF16),
                        pltpu.VMEM((N_SQUARE, d, d), BF16),
                        pltpu.VMEM((bsz, N_MOD * d), F32),
                        pltpu.VMEM((2, STAGE, STAGE), F32),
                        pltpu.SemaphoreType.DMA((2,)),
                        pltpu.VMEM((GLA_HEADS, GLA_HEAD_V, GLA_HEAD_K), F32),
                        pltpu.VMEM((POOL_HALO, d), F32),
                        pltpu.VMEM((ts, GLA_VAL_DIM), BF16),
                        pltpu.VMEM((ts, d), BF16)],
        compiler_params=pltpu.CompilerParams(dimension_semantics=("arbitrary", "arbitrary"),
                                             vmem_limit_bytes=VMEM_LIMIT_BYTES),
        name="hybrid_pool_gla_layer",
    )(x, *vmem_operands, *hbm_operands)


def kernel(x, c, g_norm, w_ada, b_ada, w_in, w_pool_group, pool_scale, w_alpha_up, b_alpha, g_gla_head,
           w_pool_out, w_gla_out, w_out, g_final):
    return _forward(x, c, g_norm, w_ada, b_ada, w_in, w_pool_group, pool_scale, w_alpha_up, b_alpha,
                    g_gla_head, w_pool_out, w_gla_out, w_out, g_final)
```

```python
import functools
import math

import jax
import jax.numpy as jnp
from jax import lax
from jax.experimental import pallas as pl
from jax.experimental.pallas import tpu as pltpu

F32 = jnp.float32
BF16 = jnp.bfloat16

D_MODEL = 1024
EPS = 1e-6
POOL_WINDOWS = (2, 4, 8, 16)
POOL_GROUP_DIM = D_MODEL // len(POOL_WINDOWS)
POOL_HALO = 16
GLA_HEADS = 4
GLA_KEY_DIM = D_MODEL // 2
GLA_VAL_DIM = D_MODEL
GLA_HEAD_K = GLA_KEY_DIM // GLA_HEADS
GLA_HEAD_V = GLA_VAL_DIM // GLA_HEADS
GLA_GATE_RANK = 16
GLA_GATE_NORMALIZER = 16.0
GLA_BLOCK = 128
LANES = 128
SUBLANES = 8
assert max(POOL_WINDOWS) <= 2 * SUBLANES <= POOL_HALO
ROW_TILE = 512
STAGE = 1024
VMEM_LIMIT_BYTES = 60 * 1024 * 1024

_IN_SEGMENTS = (("pool_value", D_MODEL), ("pool_gate", D_MODEL), ("q", GLA_KEY_DIM), ("k", GLA_KEY_DIM),
                ("decay_gate", LANES), ("v", GLA_VAL_DIM), ("gla_gate", GLA_VAL_DIM), ("merge_pool", D_MODEL),
                ("merge_gla", D_MODEL))
_IN_COLS = {}
_start = 0
for _name, _width in _IN_SEGMENTS:
    _IN_COLS[_name] = (_start, _start + _width)
    _start += _width
IN_WIDTH_PACKED = _start
_REF_GATE_COL = 2 * D_MODEL + 2 * GLA_KEY_DIM + 2 * GLA_VAL_DIM
IN_WIDTH_REF = _REF_GATE_COL + GLA_GATE_RANK + 2 * D_MODEL
_IN_CHUNKS = ((0, _IN_COLS["pool_value"][0]), (D_MODEL, _IN_COLS["pool_gate"][0]), (2 * D_MODEL, _IN_COLS["q"][0]),
              (2 * D_MODEL + 2 * GLA_KEY_DIM, _IN_COLS["v"][0]),
              (2 * D_MODEL + 2 * GLA_KEY_DIM + GLA_VAL_DIM, _IN_COLS["gla_gate"][0]),
              (_REF_GATE_COL + GLA_GATE_RANK, _IN_COLS["merge_pool"][0]),
              (_REF_GATE_COL + GLA_GATE_RANK + D_MODEL, _IN_COLS["merge_gla"][0]))
assert _IN_COLS["k"][0] == _IN_COLS["q"][0] + GLA_KEY_DIM and 2 * GLA_KEY_DIM == STAGE == D_MODEL
N_SQUARE = 3
N_MOD = 3

_NT = (((1,), (1,)), ((), ()))


def _dot(a, b):
    return jnp.dot(a, b, preferred_element_type=F32)


def _sigmoid(x):
    return 0.5 * jnp.tanh(0.5 * x) + 0.5


def _silu(x):
    half = 0.5 * x
    return half * jnp.tanh(half) + half


def _log_sigmoid(x):
    return jnp.minimum(x, 0.0) - jnp.log1p(jnp.exp(-jnp.abs(x)))


def _prepare_weights(c_ref, bada_ref, wint_hbm, wada_hbm, wsq_hbms, win_ref, wsq_ref, mod_ref, stage_ref, sem_ref):
    gate_chunk = len(_IN_CHUNKS)
    n_chunks = gate_chunk + 1 + N_SQUARE + N_MOD

    def copy(i):
        slot = i % 2
        if i < gate_chunk:
            src = wint_hbm.at[pl.ds(_IN_CHUNKS[i][0], STAGE), :]
            dst = stage_ref.at[slot]
        elif i == gate_chunk:
            src = wint_hbm.at[pl.ds(_REF_GATE_COL, LANES), :]
            dst = stage_ref.at[slot, pl.ds(0, LANES), :]
        elif i < gate_chunk + 1 + N_SQUARE:
            src = wsq_hbms[i - gate_chunk - 1]
            dst = stage_ref.at[slot]
        else:
            src = wada_hbm.at[:, pl.ds((i - gate_chunk - 1 - N_SQUARE) * STAGE, STAGE)]
            dst = stage_ref.at[slot]
        return pltpu.make_async_copy(src, dst, sem_ref.at[slot])

    silu_c = _silu(c_ref[...]).astype(BF16)
    copy(0).start()
    for i in range(n_chunks):
        if i + 1 < n_chunks:
            copy(i + 1).start()
        copy(i).wait()
        slot = i % 2
        if i < gate_chunk:
            dst0 = _IN_CHUNKS[i][1]
            win_ref[:, dst0:dst0 + STAGE] = stage_ref[slot].T.astype(BF16)
        elif i == gate_chunk:
            tile = stage_ref[slot, 0:LANES, :].T
            lane = lax.broadcasted_iota(jnp.int32, tile.shape, 1)
            g0, g1 = _IN_COLS["decay_gate"]
            win_ref[:, g0:g1] = jnp.where(lane < GLA_GATE_RANK, tile, 0.0).astype(BF16)
        elif i < gate_chunk + 1 + N_SQUARE:
            wsq_ref[i - gate_chunk - 1] = stage_ref[slot].astype(BF16)
        else:
            c0 = (i - gate_chunk - 1 - N_SQUARE) * STAGE
            mod_ref[:, c0:c0 + STAGE] = _dot(silu_c, stage_ref[slot].astype(BF16)) + bada_ref[:, c0:c0 + STAGE]


def _layer_kernel(x_ref, c_ref, bada_ref, gnorm_ref, wgrp_ref, pscale_ref, wup_ref, balpha_ref, ghead_ref, gfin_ref,
                  wint_hbm, wada_hbm, wpo_hbm, wgo_hbm, wo_hbm, out_ref,
                  win_ref, wsq_ref, mod_ref, stage_ref, sem_ref, st_ref, halo_ref, o_scr, hb_ref):
    b = pl.program_id(0)
    j = pl.program_id(1)
    ts = x_ref.shape[0]
    d = D_MODEL

    @pl.when(jnp.logical_and(b == 0, j == 0))
    def _():
        _prepare_weights(c_ref, bada_ref, wint_hbm, wada_hbm, (wpo_hbm, wgo_hbm, wo_hbm), win_ref, wsq_ref, mod_ref,
                         stage_ref, sem_ref)

    @pl.when(j == 0)
    def _():
        st_ref[...] = jnp.zeros_like(st_ref)
        halo_ref[...] = jnp.zeros_like(halo_ref)

    wpo_ref, wgo_ref, wo_ref = wsq_ref.at[0], wsq_ref.at[1], wsq_ref.at[2]
    x = x_ref[...]
    mod = mod_ref[pl.ds(b, 1), :]
    shift, scale, gate = mod[:, 0:d], mod[:, d:2 * d], mod[:, 2 * d:3 * d]
    gain = gnorm_ref[...] * (1.0 + scale)
    h = x * lax.rsqrt(jnp.mean(x * x, axis=-1, keepdims=True) + EPS) * gain + shift
    hb_ref[...] = h.astype(BF16)

    def proj(c0, c1, rows=slice(None)):
        return _dot(hb_ref[rows, :], win_ref[:, c0:c1])

    def proj_seg(name):
        return proj(*_IN_COLS[name])

    half = ts // 2
    halves = [slice(0, half), slice(half, ts)]
    k0, g1 = _IN_COLS["k"][0], _IN_COLS["decay_gate"][1]
    ka = proj(k0, g1)
    k = ka[:, 0:GLA_KEY_DIM]
    a_low = ka[:, GLA_KEY_DIM:GLA_KEY_DIM + LANES]
    wup = jnp.concatenate([wup_ref[...].astype(BF16), jnp.zeros((LANES - GLA_GATE_RANK, GLA_KEY_DIM), BF16)], axis=0)
    log_a = _log_sigmoid(_dot(a_low.astype(BF16), wup) + balpha_ref[...]) * (1.0 / GLA_GATE_NORMALIZER)
    la_hi = log_a.astype(BF16)
    la_lo = (log_a - la_hi.astype(F32)).astype(BF16)
    q = proj_seg("q")
    log_q_scale = -0.5 * math.log(GLA_HEAD_K)
    v = proj_seg("v")
    rr = lax.broadcasted_iota(jnp.int32, (GLA_BLOCK, GLA_BLOCK), 0)
    cc = lax.broadcasted_iota(jnp.int32, (GLA_BLOCK, GLA_BLOCK), 1)
    causal = rr >= cc
    tri = jnp.where(causal, 1.0, 0.0).astype(BF16)
    tri2 = jnp.concatenate([tri, tri], axis=1)
    nblk = ts // GLA_BLOCK
    cbs = []
    for r in range(nblk):
        rows = slice(r * GLA_BLOCK, (r + 1) * GLA_BLOCK)
        cbs.append(_dot(tri2, jnp.concatenate([la_hi[rows, :], la_lo[rows, :]], axis=0)))
    pv = proj_seg("pool_value")

    fillers = ["pool_gate", "merge_pool", "gla_gate", "merge_gla"]
    filled = []
    for r in range(nblk):
        rows = slice(r * GLA_BLOCK, (r + 1) * GLA_BLOCK)
        cb = cbs[r]
        b_mid = cb[GLA_BLOCK // 2 - 1:GLA_BLOCK // 2, :]
        b_end = cb[GLA_BLOCK - 1:GLA_BLOCK, :]
        q_mid = q[rows, :] * jnp.exp(cb - (b_mid - log_q_scale))
        k_mid = k[rows, :] * jnp.exp(b_mid - cb)
        q_in = q_mid.astype(BF16)
        k_in = k_mid.astype(BF16)
        q_start = (q_mid * jnp.exp(b_mid)).astype(BF16)
        k_end = (k_mid * jnp.exp(b_end - b_mid)).astype(BF16)
        block_decay = jnp.exp(b_end)
        heads = range(GLA_HEADS)
        kcs = [slice(hh * GLA_HEAD_K, (hh + 1) * GLA_HEAD_K) for hh in heads]
        vcs = [slice(hh * GLA_HEAD_V, (hh + 1) * GLA_HEAD_V) for hh in heads]
        scores = [lax.dot_general(q_in[:, kcs[hh]], k_in[:, kcs[hh]], _NT, preferred_element_type=F32) for hh in heads]
        vbs = [v[rows, vcs[hh]] for hh in heads]
        updates = [_dot(vbs[hh].T.astype(BF16), k_end[:, kcs[hh]]) for hh in heads]
        for hh in heads:
            state_t = st_ref[hh]
            o = _dot(jnp.where(causal, scores[hh], 0.0).astype(BF16), vbs[hh].astype(BF16))
            o = o + lax.dot_general(q_start[:, kcs[hh]], state_t.astype(BF16), _NT, preferred_element_type=F32)
            o_scr[rows, vcs[hh]] = o
            st_ref[hh] = state_t * block_decay[:, kcs[hh]] + updates[hh]
        if r < len(fillers):
            filled.append(proj_seg(fillers[r]))
    for name in fillers[len(filled):]:
        filled.append(proj_seg(name))
    pg, mgp, gg, mgg = filled
    pool_gate = _silu(pg).astype(BF16)
    merge_pool_gate = _sigmoid(mgp).astype(BF16)
    gla_gate = _silu(gg).astype(BF16)
    gate_gla = _sigmoid(mgg).astype(BF16)

    head_t_plus_1 = lax.broadcasted_iota(jnp.int32, (POOL_HALO, 1), 0) + (j * ts + 1)
    mixed = []
    for g, w in enumerate(POOL_WINDOWS):
        cols = slice(g * POOL_GROUP_DIM, (g + 1) * POOL_GROUP_DIM)
        win = jnp.concatenate([halo_ref[:, cols], pv[:, cols]], axis=0)
        step = 1
        while step < min(w, SUBLANES):
            win = win + pltpu.roll(win, step, axis=0)
            step *= 2
        if w > SUBLANES:
            win = win[POOL_HALO:, :] + win[POOL_HALO - SUBLANES:-SUBLANES, :]
        else:
            win = win[POOL_HALO:, :]
        head_inv_cnt = 1.0 / jnp.minimum(head_t_plus_1, w).astype(F32)
        mean = jnp.concatenate([win[:POOL_HALO, :] * head_inv_cnt, win[POOL_HALO:, :] * (1.0 / w)], axis=0)
        pooled = mean - pv[:, cols]
        mixed.append(_dot(pooled.astype(BF16), wgrp_ref[g].astype(BF16)))
    mixed = jnp.concatenate(mixed, axis=1)
    halo_ref[...] = pv[ts - POOL_HALO:, :]
    y_pool = mixed * pscale_ref[...] * pool_gate
    merged = merge_pool_gate * _dot(y_pool.astype(BF16), wpo_ref[...])

    o = o_scr[...]
    normed = []
    for hh in range(GLA_HEADS):
        oh = o[:, hh * GLA_HEAD_V:(hh + 1) * GLA_HEAD_V]
        normed.append(oh * lax.rsqrt(jnp.mean(oh * oh, axis=-1, keepdims=True) + EPS) * ghead_ref[...])
    y_gla = (jnp.concatenate(normed, axis=1) * gla_gate).astype(BF16)
    gfin = gfin_ref[...]
    pieces = [slice(0, ts - GLA_BLOCK), slice(ts - GLA_BLOCK, ts)]
    g_out = [_dot(y_gla[hs, :], wgo_ref[...]) for hs in pieces]
    for hs, g_half in zip(pieces, g_out):
        merged_half = merged[hs, :] + gate_gla[hs, :] * g_half
        y = x_ref[hs, :] + gate * _dot(merged_half.astype(BF16), wo_ref[...])
        out_ref[hs, :] = y * lax.rsqrt(jnp.mean(y * y, axis=-1, keepdims=True) + EPS) * gfin


def _resident(shape):
    return pl.BlockSpec(shape, lambda b, j: (0,) * len(shape), pipeline_mode=pl.Buffered(1))


@functools.partial(jax.jit, static_argnames=("row_tile",))
def _forward(x, c, g_norm, w_ada, b_ada, w_in, w_pool_group, pool_scale, w_alpha_up, b_alpha, g_gla_head,
             w_pool_out, w_gla_out, w_out, g_final, row_tile=ROW_TILE):
    assert g_norm.shape[0] == 1, "single-layer stack"
    bsz, seq, d = x.shape
    ts = row_tile
    assert d == D_MODEL and seq % ts == 0 and ts % (2 * GLA_BLOCK) == 0
    assert w_in.shape == (1, d, IN_WIDTH_REF) and w_ada.shape == (1, d, N_MOD * STAGE)

    row = lambda a: a.reshape(1, -1).astype(F32)
    vmem_operands = [
        c, row(b_ada), row(g_norm), w_pool_group[0], row(pool_scale), w_alpha_up[0], row(b_alpha),
        row(g_gla_head), row(g_final),
    ]
    hbm_operands = [w_in[0].T, w_ada[0], w_pool_out[0], w_gla_out[0], w_out[0]]
    in_specs = [pl.BlockSpec((None, ts, d), lambda b, j: (b, j, 0))]
    in_specs += [_resident(a.shape) for a in vmem_operands]
    in_specs += [pl.BlockSpec(memory_space=pl.ANY) for _ in hbm_operands]

    return pl.pallas_call(
        _layer_kernel,
        out_shape=jax.ShapeDtypeStruct((bsz, seq, d), x.dtype),
        grid=(bsz, seq // ts),
        in_specs=in_specs,
        out_specs=pl.BlockSpec((None, ts, d), lambda b, j: (b, j, 0)),
        scratch_shapes=[pltpu.VMEM((d, IN_WIDTH_PACKED), BF16),
                        pltpu.VMEM((N_SQUARE, d, d), BF16),
                        pltpu.VMEM((bsz, N_MOD * d), F32),
                        pltpu.VMEM((2, STAGE, STAGE), F32),
                        pltpu.SemaphoreType.DMA((2,)),
                        pltpu.VMEM((GLA_HEADS, GLA_HEAD_V, GLA_HEAD_K), F32),
                        pltpu.VMEM((POOL_HALO, d), F32),
                        pltpu.VMEM((ts, GLA_VAL_DIM), F32),
                        pltpu.VMEM((ts, d), BF16)],
        compiler_params=pltpu.CompilerParams(dimension_semantics=("arbitrary", "arbitrary"),
                                             vmem_limit_bytes=VMEM_LIMIT_BYTES),
        name="hybrid_pool_gla_layer",
    )(x, *vmem_operands, *hbm_operands)


def kernel(x, c, g_norm, w_ada, b_ada, w_in, w_pool_group, pool_scale, w_alpha_up, b_alpha, g_gla_head,
           w_pool_out, w_gla_out, w_out, g_final):
    return _forward(x, c, g_norm, w_ada, b_ada, w_in, w_pool_group, pool_scale, w_alpha_up, b_alpha,
                    g_gla_head, w_pool_out, w_gla_out, w_out, g_final)
```

```python
import functools
import math

import jax
import jax.numpy as jnp
from jax import lax
from jax.experimental import pallas as pl
from jax.experimental.pallas import tpu as pltpu

F32 = jnp.float32
BF16 = jnp.bfloat16

D_MODEL = 1024
EPS = 1e-6
POOL_WINDOWS = (2, 4, 8, 16)
POOL_GROUP_DIM = D_MODEL // len(POOL_WINDOWS)
POOL_HALO = 16
GLA_HEADS = 4
GLA_KEY_DIM = D_MODEL // 2
GLA_VAL_DIM = D_MODEL
GLA_HEAD_K = GLA_KEY_DIM // GLA_HEADS
GLA_HEAD_V = GLA_VAL_DIM // GLA_HEADS
GLA_GATE_RANK = 16
GLA_GATE_NORMALIZER = 16.0
GLA_BLOCK = 128
LANES = 128
SUBLANES = 8
assert max(POOL_WINDOWS) <= 2 * SUBLANES <= POOL_HALO
ROW_TILE = 512
STAGE = 1024
VMEM_LIMIT_BYTES = 60 * 1024 * 1024

_IN_SEGMENTS = (("pool_value", D_MODEL), ("pool_gate", D_MODEL), ("q", GLA_KEY_DIM), ("k", GLA_KEY_DIM),
                ("decay_gate", LANES), ("v", GLA_VAL_DIM), ("gla_gate", GLA_VAL_DIM), ("merge_pool", D_MODEL),
                ("merge_gla", D_MODEL))
_IN_COLS = {}
_start = 0
for _name, _width in _IN_SEGMENTS:
    _IN_COLS[_name] = (_start, _start + _width)
    _start += _width
IN_WIDTH_PACKED = _start
_REF_GATE_COL = 2 * D_MODEL + 2 * GLA_KEY_DIM + 2 * GLA_VAL_DIM
IN_WIDTH_REF = _REF_GATE_COL + GLA_GATE_RANK + 2 * D_MODEL
_IN_CHUNKS = ((0, _IN_COLS["pool_value"][0]), (D_MODEL, _IN_COLS["pool_gate"][0]), (2 * D_MODEL, _IN_COLS["q"][0]),
              (2 * D_MODEL + 2 * GLA_KEY_DIM, _IN_COLS["v"][0]),
              (2 * D_MODEL + 2 * GLA_KEY_DIM + GLA_VAL_DIM, _IN_COLS["gla_gate"][0]),
              (_REF_GATE_COL + GLA_GATE_RANK, _IN_COLS["merge_pool"][0]),
              (_REF_GATE_COL + GLA_GATE_RANK + D_MODEL, _IN_COLS["merge_gla"][0]))
assert _IN_COLS["k"][0] == _IN_COLS["q"][0] + GLA_KEY_DIM and 2 * GLA_KEY_DIM == STAGE == D_MODEL
N_SQUARE = 3
N_MOD = 3
_PARAM_WIDTHS = (N_MOD * D_MODEL, D_MODEL, D_MODEL, GLA_KEY_DIM, GLA_HEAD_V, D_MODEL)

_NT = (((1,), (1,)), ((), ()))


def _dot(a, b):
    return jnp.dot(a, b, preferred_element_type=F32)


def _sigmoid(x):
    return 0.5 * jnp.tanh(0.5 * x) + 0.5


def _silu(x):
    half = 0.5 * x
    return half * jnp.tanh(half) + half


def _log_sigmoid(x):
    return jnp.minimum(x, 0.0) - jnp.log1p(jnp.exp(-jnp.abs(x)))


def _prepare_weights(c_ref, row_hbms, par_ref, wint_hbm, wada_hbm, wsq_hbms, win_ref, wsq_ref, mod_ref, stage_ref,
                     sem_ref):
    gate_chunk = len(_IN_CHUNKS)
    n_chunks = gate_chunk + 1 + N_SQUARE + N_MOD

    def copy(i):
        slot = i % 2
        if i < gate_chunk:
            src = wint_hbm.at[pl.ds(_IN_CHUNKS[i][0], STAGE), :]
            dst = stage_ref.at[slot]
        elif i == gate_chunk:
            src = wint_hbm.at[pl.ds(_REF_GATE_COL, LANES), :]
            dst = stage_ref.at[slot, pl.ds(0, LANES), :]
        elif i < gate_chunk + 1 + N_SQUARE:
            src = wsq_hbms[i - gate_chunk - 1]
            dst = stage_ref.at[slot]
        else:
            src = wada_hbm.at[:, pl.ds((i - gate_chunk - 1 - N_SQUARE) * STAGE, STAGE)]
            dst = stage_ref.at[slot]
        return pltpu.make_async_copy(src, dst, sem_ref.at[slot])

    for r, (src, width) in enumerate(zip(row_hbms, _PARAM_WIDTHS)):
        row_copy = pltpu.make_async_copy(src, par_ref.at[pl.ds(r, 1), pl.ds(0, width)], sem_ref.at[2])
        row_copy.start()
        row_copy.wait()
    silu_c = _silu(c_ref[...]).astype(BF16)
    copy(0).start()
    for i in range(n_chunks):
        if i + 1 < n_chunks:
            copy(i + 1).start()
        copy(i).wait()
        slot = i % 2
        if i < gate_chunk:
            dst0 = _IN_CHUNKS[i][1]
            win_ref[:, dst0:dst0 + STAGE] = stage_ref[slot].T.astype(BF16)
        elif i == gate_chunk:
            tile = stage_ref[slot, 0:LANES, :].T
            lane = lax.broadcasted_iota(jnp.int32, tile.shape, 1)
            g0, g1 = _IN_COLS["decay_gate"]
            win_ref[:, g0:g1] = jnp.where(lane < GLA_GATE_RANK, tile, 0.0).astype(BF16)
        elif i < gate_chunk + 1 + N_SQUARE:
            wsq_ref[i - gate_chunk - 1] = stage_ref[slot].astype(BF16)
        else:
            c0 = (i - gate_chunk - 1 - N_SQUARE) * STAGE
            mod_ref[:, c0:c0 + STAGE] = _dot(silu_c, stage_ref[slot].astype(BF16)) + par_ref[0:1, c0:c0 + STAGE]


def _layer_kernel(x_ref, c_ref, wgrp_ref, wup_ref, wint_hbm, wada_hbm, wpo_hbm, wgo_hbm, wo_hbm,
                  bada_hbm, gnorm_hbm, pscale_hbm, balpha_hbm, ghead_hbm, gfin_hbm, out_ref,
                  win_ref, wsq_ref, mod_ref, stage_ref, sem_ref, st_ref, halo_ref, o_scr, hb_ref, par_ref):
    b = pl.program_id(0)
    j = pl.program_id(1)
    ts = x_ref.shape[0]
    d = D_MODEL

    @pl.when(jnp.logical_and(b == 0, j == 0))
    def _():
        _prepare_weights(c_ref, (bada_hbm, gnorm_hbm, pscale_hbm, balpha_hbm, ghead_hbm, gfin_hbm), par_ref,
                         wint_hbm, wada_hbm, (wpo_hbm, wgo_hbm, wo_hbm), win_ref, wsq_ref, mod_ref, stage_ref, sem_ref)

    @pl.when(j == 0)
    def _():
        st_ref[...] = jnp.zeros_like(st_ref)
        halo_ref[...] = jnp.zeros_like(halo_ref)

    wpo_ref, wgo_ref, wo_ref = wsq_ref.at[0], wsq_ref.at[1], wsq_ref.at[2]
    x = x_ref[...]
    mod = mod_ref[pl.ds(b, 1), :]
    shift, scale, gate = mod[:, 0:d], mod[:, d:2 * d], mod[:, 2 * d:3 * d]
    gain = par_ref[1:2, 0:d] * (1.0 + scale)
    h = x * lax.rsqrt(jnp.mean(x * x, axis=-1, keepdims=True) + EPS) * gain + shift
    hb_ref[...] = h.astype(BF16)

    def proj(c0, c1, rows=slice(None)):
        return _dot(hb_ref[rows, :], win_ref[:, c0:c1])

    def proj_seg(name):
        return proj(*_IN_COLS[name])

    half = ts // 2
    halves = [slice(0, half), slice(half, ts)]
    k0, g1 = _IN_COLS["k"][0], _IN_COLS["decay_gate"][1]
    ka = proj(k0, g1)
    k = ka[:, 0:GLA_KEY_DIM]
    a_low = ka[:, GLA_KEY_DIM:GLA_KEY_DIM + LANES]
    wup = jnp.concatenate([wup_ref[...].astype(BF16), jnp.zeros((LANES - GLA_GATE_RANK, GLA_KEY_DIM), BF16)], axis=0)
    b_alpha = par_ref[3:4, 0:GLA_KEY_DIM]
    log_a = _log_sigmoid(_dot(a_low.astype(BF16), wup) + b_alpha) * (1.0 / GLA_GATE_NORMALIZER)
    la_hi = log_a.astype(BF16)
    la_lo = (log_a - la_hi.astype(F32)).astype(BF16)
    q = proj_seg("q")
    log_q_scale = -0.5 * math.log(GLA_HEAD_K)
    v = proj_seg("v")
    rr = lax.broadcasted_iota(jnp.int32, (GLA_BLOCK, GLA_BLOCK), 0)
    cc = lax.broadcasted_iota(jnp.int32, (GLA_BLOCK, GLA_BLOCK), 1)
    causal = rr >= cc
    tri = jnp.where(causal, 1.0, 0.0).astype(BF16)
    tri2 = jnp.concatenate([tri, tri], axis=1)
    nblk = ts // GLA_BLOCK
    cbs = []
    for r in range(nblk):
        rows = slice(r * GLA_BLOCK, (r + 1) * GLA_BLOCK)
        cbs.append(_dot(tri2, jnp.concatenate([la_hi[rows, :], la_lo[rows, :]], axis=0)))
    pv = proj_seg("pool_value")

    fillers = ["pool_gate", "merge_pool", "gla_gate", "merge_gla"]
    filled = []
    for r in range(nblk):
        rows = slice(r * GLA_BLOCK, (r + 1) * GLA_BLOCK)
        cb = cbs[r]
        b_mid = cb[GLA_BLOCK // 2 - 1:GLA_BLOCK // 2, :]
        b_end = cb[GLA_BLOCK - 1:GLA_BLOCK, :]
        q_mid = q[rows, :] * jnp.exp(cb - (b_mid - log_q_scale))
        k_mid = k[rows, :] * jnp.exp(b_mid - cb)
        q_in = q_mid.astype(BF16)
        k_in = k_mid.astype(BF16)
        q_start = (q_mid * jnp.exp(b_mid)).astype(BF16)
        k_end = (k_mid * jnp.exp(b_end - b_mid)).astype(BF16)
        block_decay = jnp.exp(b_end)
        heads = range(GLA_HEADS)
        kcs = [slice(hh * GLA_HEAD_K, (hh + 1) * GLA_HEAD_K) for hh in heads]
        vcs = [slice(hh * GLA_HEAD_V, (hh + 1) * GLA_HEAD_V) for hh in heads]
        scores = [lax.dot_general(q_in[:, kcs[hh]], k_in[:, kcs[hh]], _NT, preferred_element_type=F32) for hh in heads]
        vbs = [v[rows, vcs[hh]] for hh in heads]
        updates = [_dot(vbs[hh].T.astype(BF16), k_end[:, kcs[hh]]) for hh in heads]
        for hh in heads:
            state_t = st_ref[hh]
            o = _dot(jnp.where(causal, scores[hh], 0.0).astype(BF16), vbs[hh].astype(BF16))
            o = o + lax.dot_general(q_start[:, kcs[hh]], state_t.astype(BF16), _NT, preferred_element_type=F32)
            o_scr[rows, vcs[hh]] = o
            st_ref[hh] = state_t * block_decay[:, kcs[hh]] + updates[hh]
        if r < len(fillers):
            filled.append(proj_seg(fillers[r]))
    for name in fillers[len(filled):]:
        filled.append(proj_seg(name))
    pg, mgp, gg, mgg = filled
    pool_gate = _silu(pg).astype(BF16)
    merge_pool_gate = _sigmoid(mgp).astype(BF16)
    gla_gate = _silu(gg).astype(BF16)
    gate_gla = _sigmoid(mgg).astype(BF16)

    head_t_plus_1 = lax.broadcasted_iota(jnp.int32, (POOL_HALO, 1), 0) + (j * ts + 1)
    mixed = []
    for g, w in enumerate(POOL_WINDOWS):
        cols = slice(g * POOL_GROUP_DIM, (g + 1) * POOL_GROUP_DIM)
        win = jnp.concatenate([halo_ref[:, cols], pv[:, cols]], axis=0)
        step = 1
        while step < min(w, SUBLANES):
            win = win + pltpu.roll(win, step, axis=0)
            step *= 2
        if w > SUBLANES:
            win = win[POOL_HALO:, :] + win[POOL_HALO - SUBLANES:-SUBLANES, :]
        else:
            win = win[POOL_HALO:, :]
        head_inv_cnt = 1.0 / jnp.minimum(head_t_plus_1, w).astype(F32)
        mean = jnp.concatenate([win[:POOL_HALO, :] * head_inv_cnt, win[POOL_HALO:, :] * (1.0 / w)], axis=0)
        pooled = mean - pv[:, cols]
        mixed.append(_dot(pooled.astype(BF16), wgrp_ref[g].astype(BF16)))
    mixed = jnp.concatenate(mixed, axis=1)
    halo_ref[...] = pv[ts - POOL_HALO:, :]
    y_pool = mixed * par_ref[2:3, 0:d] * pool_gate
    merged = merge_pool_gate * _dot(y_pool.astype(BF16), wpo_ref[...])

    o = o_scr[...]
    normed = []
    for hh in range(GLA_HEADS):
        oh = o[:, hh * GLA_HEAD_V:(hh + 1) * GLA_HEAD_V]
        normed.append(oh * lax.rsqrt(jnp.mean(oh * oh, axis=-1, keepdims=True) + EPS) * par_ref[4:5, 0:GLA_HEAD_V])
    y_gla = (jnp.concatenate(normed, axis=1) * gla_gate).astype(BF16)
    gfin = par_ref[5:6, 0:d]
    g_out = [_dot(y_gla[hs, :], wgo_ref[...]) for hs in halves]
    for hs, g_half in zip(halves, g_out):
        merged_half = merged[hs, :] + gate_gla[hs, :] * g_half
        y = x_ref[hs, :] + gate * _dot(merged_half.astype(BF16), wo_ref[...])
        out_ref[hs, :] = y * lax.rsqrt(jnp.mean(y * y, axis=-1, keepdims=True) + EPS) * gfin


def _resident(shape):
    return pl.BlockSpec(shape, lambda b, j: (0,) * len(shape), pipeline_mode=pl.Buffered(1))


@functools.partial(jax.jit, static_argnames=("row_tile",))
def _forward(x, c, g_norm, w_ada, b_ada, w_in, w_pool_group, pool_scale, w_alpha_up, b_alpha, g_gla_head,
             w_pool_out, w_gla_out, w_out, g_final, row_tile=ROW_TILE):
    assert g_norm.shape[0] == 1, "single-layer stack"
    bsz, seq, d = x.shape
    ts = row_tile
    assert d == D_MODEL and seq % ts == 0 and ts % (2 * GLA_BLOCK) == 0
    assert w_in.shape == (1, d, IN_WIDTH_REF) and w_ada.shape == (1, d, N_MOD * STAGE)

    row = lambda a: a.reshape(1, -1).astype(F32)
    vmem_operands = [c, w_pool_group[0], w_alpha_up[0]]
    hbm_operands = [w_in[0].T, w_ada[0], w_pool_out[0], w_gla_out[0], w_out[0],
                    row(b_ada), row(g_norm), row(pool_scale), row(b_alpha), row(g_gla_head), row(g_final)]
    in_specs = [pl.BlockSpec((None, ts, d), lambda b, j: (b, j, 0))]
    in_specs += [_resident(a.shape) for a in vmem_operands]
    in_specs += [pl.BlockSpec(memory_space=pl.ANY) for _ in hbm_operands]

    return pl.pallas_call(
        _layer_kernel,
        out_shape=jax.ShapeDtypeStruct((bsz, seq, d), x.dtype),
        grid=(bsz, seq // ts),
        in_specs=in_specs,
        out_specs=pl.BlockSpec((None, ts, d), lambda b, j: (b, j, 0)),
        scratch_shapes=[pltpu.VMEM((d, IN_WIDTH_PACKED), BF16),
                        pltpu.VMEM((N_SQUARE, d, d), BF16),
                        pltpu.VMEM((bsz, N_MOD * d), F32),
                        pltpu.VMEM((2, STAGE, STAGE), F32),
                        pltpu.SemaphoreType.DMA((3,)),
                        pltpu.VMEM((GLA_HEADS, GLA_HEAD_V, GLA_HEAD_K), F32),
                        pltpu.VMEM((POOL_HALO, d), F32),
                        pltpu.VMEM((ts, GLA_VAL_DIM), F32),
                        pltpu.VMEM((ts, d), BF16),
                        pltpu.VMEM((SUBLANES, N_MOD * d), F32)],
        compiler_params=pltpu.CompilerParams(dimension_semantics=("arbitrary", "arbitrary"),
                                             vmem_limit_bytes=VMEM_LIMIT_BYTES),
        name="hybrid_pool_gla_layer",
    )(x, *vmem_operands, *hbm_operands)


def kernel(x, c, g_norm, w_ada, b_ada, w_in, w_pool_group, pool_scale, w_alpha_up, b_alpha, g_gla_head,
           w_pool_out, w_gla_out, w_out, g_final):
    return _forward(x, c, g_norm, w_ada, b_ada, w_in, w_pool_group, pool_scale, w_alpha_up, b_alpha,
                    g_gla_head, w_pool_out, w_gla_out, w_out, g_final)
```

```python
import functools
import math

import jax
import jax.numpy as jnp
from jax import lax
from jax.experimental import pallas as pl
from jax.experimental.pallas import tpu as pltpu

F32 = jnp.float32
BF16 = jnp.bfloat16

D_MODEL = 1024
EPS = 1e-6
POOL_WINDOWS = (2, 4, 8, 16)
POOL_GROUP_DIM = D_MODEL // len(POOL_WINDOWS)
POOL_HALO = 16
GLA_HEADS = 4
GLA_KEY_DIM = D_MODEL // 2
GLA_VAL_DIM = D_MODEL
GLA_HEAD_K = GLA_KEY_DIM // GLA_HEADS
GLA_HEAD_V = GLA_VAL_DIM // GLA_HEADS
GLA_GATE_RANK = 16
GLA_GATE_NORMALIZER = 16.0
GLA_BLOCK = 128
LANES = 128
SUBLANES = 8
assert max(POOL_WINDOWS) <= 2 * SUBLANES <= POOL_HALO
ROW_TILE = 256
STAGE = 1024
VMEM_LIMIT_BYTES = 60 * 1024 * 1024

_IN_SEGMENTS = (("pool_value", D_MODEL), ("pool_gate", D_MODEL), ("q", GLA_KEY_DIM), ("k", GLA_KEY_DIM),
                ("decay_gate", LANES), ("v", GLA_VAL_DIM), ("gla_gate", GLA_VAL_DIM), ("merge_pool", D_MODEL),
                ("merge_gla", D_MODEL))
_IN_COLS = {}
_start = 0
for _name, _width in _IN_SEGMENTS:
    _IN_COLS[_name] = (_start, _start + _width)
    _start += _width
IN_WIDTH_PACKED = _start
_REF_GATE_COL = 2 * D_MODEL + 2 * GLA_KEY_DIM + 2 * GLA_VAL_DIM
IN_WIDTH_REF = _REF_GATE_COL + GLA_GATE_RANK + 2 * D_MODEL
_IN_CHUNKS = ((0, _IN_COLS["pool_value"][0]), (D_MODEL, _IN_COLS["pool_gate"][0]), (2 * D_MODEL, _IN_COLS["q"][0]),
              (2 * D_MODEL + 2 * GLA_KEY_DIM, _IN_COLS["v"][0]),
              (2 * D_MODEL + 2 * GLA_KEY_DIM + GLA_VAL_DIM, _IN_COLS["gla_gate"][0]),
              (_REF_GATE_COL + GLA_GATE_RANK, _IN_COLS["merge_pool"][0]),
              (_REF_GATE_COL + GLA_GATE_RANK + D_MODEL, _IN_COLS["merge_gla"][0]))
assert _IN_COLS["k"][0] == _IN_COLS["q"][0] + GLA_KEY_DIM and 2 * GLA_KEY_DIM == STAGE == D_MODEL
N_SQUARE = 3
N_MOD = 3

_NT = (((1,), (1,)), ((), ()))


def _dot(a, b):
    return jnp.dot(a, b, preferred_element_type=F32)


def _sigmoid(x):
    return 0.5 * jnp.tanh(0.5 * x) + 0.5


def _silu(x):
    half = 0.5 * x
    return half * jnp.tanh(half) + half


def _log_sigmoid(x):
    return jnp.minimum(x, 0.0) - jnp.log1p(jnp.exp(-jnp.abs(x)))


def _prepare_weights(c_ref, bada_ref, wint_hbm, wada_hbm, wsq_hbms, win_ref, wsq_ref, mod_ref, stage_ref, sem_ref):
    gate_chunk = len(_IN_CHUNKS)
    n_chunks = gate_chunk + 1 + N_SQUARE + N_MOD

    def copy(i):
        slot = i % 2
        if i < gate_chunk:
            src = wint_hbm.at[pl.ds(_IN_CHUNKS[i][0], STAGE), :]
            dst = stage_ref.at[slot]
        elif i == gate_chunk:
            src = wint_hbm.at[pl.ds(_REF_GATE_COL, LANES), :]
            dst = stage_ref.at[slot, pl.ds(0, LANES), :]
        elif i < gate_chunk + 1 + N_SQUARE:
            src = wsq_hbms[i - gate_chunk - 1]
            dst = stage_ref.at[slot]
        else:
            src = wada_hbm.at[:, pl.ds((i - gate_chunk - 1 - N_SQUARE) * STAGE, STAGE)]
            dst = stage_ref.at[slot]
        return pltpu.make_async_copy(src, dst, sem_ref.at[slot])

    silu_c = _silu(c_ref[...]).astype(BF16)
    copy(0).start()
    for i in range(n_chunks):
        if i + 1 < n_chunks:
            copy(i + 1).start()
        copy(i).wait()
        slot = i % 2
        if i < gate_chunk:
            dst0 = _IN_CHUNKS[i][1]
            win_ref[:, dst0:dst0 + STAGE] = stage_ref[slot].T.astype(BF16)
        elif i == gate_chunk:
            tile = stage_ref[slot, 0:LANES, :].T
            lane = lax.broadcasted_iota(jnp.int32, tile.shape, 1)
            g0, g1 = _IN_COLS["decay_gate"]
            win_ref[:, g0:g1] = jnp.where(lane < GLA_GATE_RANK, tile, 0.0).astype(BF16)
        elif i < gate_chunk + 1 + N_SQUARE:
            wsq_ref[i - gate_chunk - 1] = stage_ref[slot].astype(BF16)
        else:
            c0 = (i - gate_chunk - 1 - N_SQUARE) * STAGE
            mod_ref[:, c0:c0 + STAGE] = _dot(silu_c, stage_ref[slot].astype(BF16)) + bada_ref[:, c0:c0 + STAGE]


def _layer_kernel(x_ref, c_ref, bada_ref, gnorm_ref, wgrp_ref, pscale_ref, wup_ref, balpha_ref, ghead_ref, gfin_ref,
                  wint_hbm, wada_hbm, wpo_hbm, wgo_hbm, wo_hbm, out_ref,
                  win_ref, wsq_ref, mod_ref, stage_ref, sem_ref, st_ref, halo_ref, o_scr, hb_ref):
    b = pl.program_id(0)
    j = pl.program_id(1)
    ts = x_ref.shape[0]
    d = D_MODEL

    @pl.when(jnp.logical_and(b == 0, j == 0))
    def _():
        _prepare_weights(c_ref, bada_ref, wint_hbm, wada_hbm, (wpo_hbm, wgo_hbm, wo_hbm), win_ref, wsq_ref, mod_ref,
                         stage_ref, sem_ref)

    @pl.when(j == 0)
    def _():
        st_ref[...] = jnp.zeros_like(st_ref)
        halo_ref[...] = jnp.zeros_like(halo_ref)

    wpo_ref, wgo_ref, wo_ref = wsq_ref.at[0], wsq_ref.at[1], wsq_ref.at[2]
    x = x_ref[...]
    mod = mod_ref[pl.ds(b, 1), :]
    shift, scale, gate = mod[:, 0:d], mod[:, d:2 * d], mod[:, 2 * d:3 * d]
    gain = gnorm_ref[...] * (1.0 + scale)
    h = x * lax.rsqrt(jnp.mean(x * x, axis=-1, keepdims=True) + EPS) * gain + shift
    hb_ref[...] = h.astype(BF16)

    def proj(c0, c1, rows=slice(None)):
        return _dot(hb_ref[rows, :], win_ref[:, c0:c1])

    def proj_seg(name):
        return proj(*_IN_COLS[name])

    half = ts // 2
    halves = [slice(0, half), slice(half, ts)]
    k0, g1 = _IN_COLS["k"][0], _IN_COLS["decay_gate"][1]
    ka = proj(k0, g1)
    k = ka[:, 0:GLA_KEY_DIM]
    a_low = ka[:, GLA_KEY_DIM:GLA_KEY_DIM + LANES]
    wup = jnp.concatenate([wup_ref[...].astype(BF16), jnp.zeros((LANES - GLA_GATE_RANK, GLA_KEY_DIM), BF16)], axis=0)
    log_a = _log_sigmoid(_dot(a_low.astype(BF16), wup) + balpha_ref[...]) * (1.0 / GLA_GATE_NORMALIZER)
    la_hi = log_a.astype(BF16)
    la_lo = (log_a - la_hi.astype(F32)).astype(BF16)
    q = proj_seg("q")
    log_q_scale = -0.5 * math.log(GLA_HEAD_K)
    v = proj_seg("v")
    rr = lax.broadcasted_iota(jnp.int32, (GLA_BLOCK, GLA_BLOCK), 0)
    cc = lax.broadcasted_iota(jnp.int32, (GLA_BLOCK, GLA_BLOCK), 1)
    causal = rr >= cc
    tri = jnp.where(causal, 1.0, 0.0).astype(BF16)
    tri2 = jnp.concatenate([tri, tri], axis=1)
    nblk = ts // GLA_BLOCK
    cbs = []
    for r in range(nblk):
        rows = slice(r * GLA_BLOCK, (r + 1) * GLA_BLOCK)
        cbs.append(_dot(tri2, jnp.concatenate([la_hi[rows, :], la_lo[rows, :]], axis=0)))
    pv = proj_seg("pool_value")

    fillers = ["pool_gate", "merge_pool", "gla_gate", "merge_gla"]
    filled = []
    for r in range(nblk):
        rows = slice(r * GLA_BLOCK, (r + 1) * GLA_BLOCK)
        cb = cbs[r]
        b_mid = cb[GLA_BLOCK // 2 - 1:GLA_BLOCK // 2, :]
        b_end = cb[GLA_BLOCK - 1:GLA_BLOCK, :]
        q_mid = q[rows, :] * jnp.exp(cb - (b_mid - log_q_scale))
        k_mid = k[rows, :] * jnp.exp(b_mid - cb)
        q_in = q_mid.astype(BF16)
        k_in = k_mid.astype(BF16)
        q_start = (q_mid * jnp.exp(b_mid)).astype(BF16)
        k_end = (k_mid * jnp.exp(b_end - b_mid)).astype(BF16)
        block_decay = jnp.exp(b_end)
        heads = range(GLA_HEADS)
        kcs = [slice(hh * GLA_HEAD_K, (hh + 1) * GLA_HEAD_K) for hh in heads]
        vcs = [slice(hh * GLA_HEAD_V, (hh + 1) * GLA_HEAD_V) for hh in heads]
        scores = [lax.dot_general(q_in[:, kcs[hh]], k_in[:, kcs[hh]], _NT, preferred_element_type=F32) for hh in heads]
        vbs = [v[rows, vcs[hh]] for hh in heads]
        updates = [_dot(vbs[hh].T.astype(BF16), k_end[:, kcs[hh]]) for hh in heads]
        for hh in heads:
            state_t = st_ref[hh]
            o = _dot(jnp.where(causal, scores[hh], 0.0).astype(BF16), vbs[hh].astype(BF16))
            o = o + lax.dot_general(q_start[:, kcs[hh]], state_t.astype(BF16), _NT, preferred_element_type=F32)
            o_scr[rows, vcs[hh]] = o
            st_ref[hh] = state_t * block_decay[:, kcs[hh]] + updates[hh]
        if r < len(fillers):
            filled.append(proj_seg(fillers[r]))
    for name in fillers[len(filled):]:
        filled.append(proj_seg(name))
    pg, mgp, gg, mgg = filled
    pool_gate = _silu(pg).astype(BF16)
    merge_pool_gate = _sigmoid(mgp).astype(BF16)
    gla_gate = _silu(gg).astype(BF16)
    gate_gla = _sigmoid(mgg).astype(BF16)

    head_t_plus_1 = lax.broadcasted_iota(jnp.int32, (POOL_HALO, 1), 0) + (j * ts + 1)
    mixed = []
    for g, w in enumerate(POOL_WINDOWS):
        cols = slice(g * POOL_GROUP_DIM, (g + 1) * POOL_GROUP_DIM)
        win = jnp.concatenate([halo_ref[:, cols], pv[:, cols]], axis=0)
        step = 1
        while step < min(w, SUBLANES):
            win = win + pltpu.roll(win, step, axis=0)
            step *= 2
        if w > SUBLANES:
            win = win[POOL_HALO:, :] + win[POOL_HALO - SUBLANES:-SUBLANES, :]
        else:
            win = win[POOL_HALO:, :]
        head_inv_cnt = 1.0 / jnp.minimum(head_t_plus_1, w).astype(F32)
        mean = jnp.concatenate([win[:POOL_HALO, :] * head_inv_cnt, win[POOL_HALO:, :] * (1.0 / w)], axis=0)
        pooled = mean - pv[:, cols]
        mixed.append(_dot(pooled.astype(BF16), wgrp_ref[g].astype(BF16)))
    mixed = jnp.concatenate(mixed, axis=1)
    halo_ref[...] = pv[ts - POOL_HALO:, :]
    y_pool = mixed * pscale_ref[...] * pool_gate
    merged = merge_pool_gate * _dot(y_pool.astype(BF16), wpo_ref[...])

    o = o_scr[...]
    normed = []
    for hh in range(GLA_HEADS):
        oh = o[:, hh * GLA_HEAD_V:(hh + 1) * GLA_HEAD_V]
        normed.append(oh * lax.rsqrt(jnp.mean(oh * oh, axis=-1, keepdims=True) + EPS) * ghead_ref[...])
    y_gla = (jnp.concatenate(normed, axis=1) * gla_gate).astype(BF16)
    gfin = gfin_ref[...]
    g_out = [_dot(y_gla[hs, :], wgo_ref[...]) for hs in halves]
    for hs, g_half in zip(halves, g_out):
        merged_half = merged[hs, :] + gate_gla[hs, :] * g_half
        y = x_ref[hs, :] + gate * _dot(merged_half.astype(BF16), wo_ref[...])
        out_ref[hs, :] = y * lax.rsqrt(jnp.mean(y * y, axis=-1, keepdims=True) + EPS) * gfin


def _resident(shape):
    return pl.BlockSpec(shape, lambda b, j: (0,) * len(shape), pipeline_mode=pl.Buffered(1))


@functools.partial(jax.jit, static_argnames=("row_tile",))
def _forward(x, c, g_norm, w_ada, b_ada, w_in, w_pool_group, pool_scale, w_alpha_up, b_alpha, g_gla_head,
             w_pool_out, w_gla_out, w_out, g_final, row_tile=ROW_TILE):
    assert g_norm.shape[0] == 1, "single-layer stack"
    bsz, seq, d = x.shape
    ts = row_tile
    assert d == D_MODEL and seq % ts == 0 and ts % (2 * GLA_BLOCK) == 0
    assert w_in.shape == (1, d, IN_WIDTH_REF) and w_ada.shape == (1, d, N_MOD * STAGE)

    row = lambda a: a.reshape(1, -1).astype(F32)
    vmem_operands = [
        c, row(b_ada), row(g_norm), w_pool_group[0], row(pool_scale), w_alpha_up[0], row(b_alpha),
        row(g_gla_head), row(g_final),
    ]
    hbm_operands = [w_in[0].T, w_ada[0], w_pool_out[0], w_gla_out[0], w_out[0]]
    in_specs = [pl.BlockSpec((None, ts, d), lambda b, j: (b, j, 0))]
    in_specs += [_resident(a.shape) for a in vmem_operands]
    in_specs += [pl.BlockSpec(memory_space=pl.ANY) for _ in hbm_operands]

    return pl.pallas_call(
        _layer_kernel,
        out_shape=jax.ShapeDtypeStruct((bsz, seq, d), x.dtype),
        grid=(bsz, seq // ts),
        in_specs=in_specs,
        out_specs=pl.BlockSpec((None, ts, d), lambda b, j: (b, j, 0)),
        scratch_shapes=[pltpu.VMEM((d, IN_WIDTH_PACKED), BF16),
                        pltpu.VMEM((N_SQUARE, d, d), BF16),
                        pltpu.VMEM((bsz, N_MOD * d), F32),
                        pltpu.VMEM((2, STAGE, STAGE), F32),
                        pltpu.SemaphoreType.DMA((2,)),
                        pltpu.VMEM((GLA_HEADS, GLA_HEAD_V, GLA_HEAD_K), F32),
                        pltpu.VMEM((POOL_HALO, d), F32),
                        pltpu.VMEM((ts, GLA_VAL_DIM), F32),
                        pltpu.VMEM((ts, d), BF16)],
        compiler_params=pltpu.CompilerParams(dimension_semantics=("arbitrary", "arbitrary"),
                                             vmem_limit_bytes=VMEM_LIMIT_BYTES),
        name="hybrid_pool_gla_layer",
    )(x, *vmem_operands, *hbm_operands)


def kernel(x, c, g_norm, w_ada, b_ada, w_in, w_pool_group, pool_scale, w_alpha_up, b_alpha, g_gla_head,
           w_pool_out, w_gla_out, w_out, g_final):
    return _forward(x, c, g_norm, w_ada, b_ada, w_in, w_pool_group, pool_scale, w_alpha_up, b_alpha,
                    g_gla_head, w_pool_out, w_gla_out, w_out, g_final)
```
